```python
import jax, jax.numpy as jnp
from jax import lax
import numpy as np


D_MODEL = 1024
BATCH = 4
SEQ = 4096
DEPTH = 4
DEC_BATCH = 128
DEC_SEQ = 8
PAST_LEN = 2048
PAGE_SIZE = 128

N_MIXERS = 2
N_SB_LAYERS = (DEPTH + N_MIXERS - 1) // N_MIXERS
N_DIL_LAYERS = DEPTH // N_MIXERS
HEAD_DIM = 128
SB_HEADS = D_MODEL // HEAD_DIM
SB_WIDTH = SB_HEADS * HEAD_DIM
SB_BIAS_INIT = -7.0
DIL_GROUPS = ((128, 1), (512, 4), (2048, 16))
N_DIL_GROUPS = 3
DIL_HEADS_PER_GROUP = 4
DIL_HEADS = N_DIL_GROUPS * DIL_HEADS_PER_GROUP
DIL_QKV_WIDTH = DIL_HEADS * HEAD_DIM
DIL_OUT_WIDTH = DIL_HEADS_PER_GROUP * HEAD_DIM
QUERY_BLOCK = 128
ALIBI_MAX_EXP = 8.0
EPS = 1e-6

kernel_name = 'stickbreak_dilated_hybrid_step'


def rmsnorm(x, g):
    xf = x.astype(jnp.float32)
    xf = xf * lax.rsqrt(jnp.mean(xf * xf, axis=-1, keepdims=True) + EPS)
    return (xf * g.astype(jnp.float32)).astype(x.dtype)


def gated_out(o, gate, w_out):
    return (o.astype(gate.dtype) * jax.nn.silu(gate)) @ w_out


def sb_project(h, w_in):
    b, t, _ = h.shape
    q, k, v, gate = jnp.split(h @ w_in, 4, axis=-1)
    shp = (b, t, SB_HEADS, HEAD_DIM)
    return q.reshape(shp), k.reshape(shp), v.reshape(shp), gate


def sb_attend(q, k, v, bias, q_pos, k_pos):
    z = jnp.einsum('bqhd,bkhd->bhqk', q, k).astype(jnp.float32) * (HEAD_DIM ** -0.5)
    z = z + bias.astype(jnp.float32)[None, :, None, None]
    strict = (k_pos[None, :] < q_pos[:, None])[None, None]
    log_keep = jnp.where(strict, jax.nn.log_sigmoid(-z), 0.0)
    log_after = lax.cumsum(log_keep, axis=3, reverse=True) - log_keep
    a = jnp.where(strict, jnp.exp(jax.nn.log_sigmoid(z) + log_after), 0.0)
    return jnp.einsum('bhqk,bkhd->bqhd', a, v.astype(jnp.float32))


def sb_prompt(h, w_in, bias, w_out):
    b, s, _ = h.shape
    q, k, v, gate = sb_project(h, w_in)
    k_pos = jnp.arange(s)

    def block(t0):
        q_blk = lax.dynamic_slice_in_dim(q, t0, QUERY_BLOCK, axis=1)
        return sb_attend(q_blk, k, v, bias, t0 + jnp.arange(QUERY_BLOCK), k_pos)

    o = lax.map(block, jnp.arange(0, s, QUERY_BLOCK))
    o = jnp.moveaxis(o, 0, 1).reshape(b, s, SB_WIDTH)
    return gated_out(o, gate, w_out), jnp.stack([k, v], axis=2)


def sb_sample(h, past_kv, w_in, bias, w_out):
    b, t, _ = h.shape
    p = past_kv.shape[1]
    q, k, v, gate = sb_project(h, w_in)
    k_all = jnp.concatenate([past_kv[:, :, 0].astype(k.dtype), k], axis=1)
    v_all = jnp.concatenate([past_kv[:, :, 1].astype(v.dtype), v], axis=1)
    o = sb_attend(q, k_all, v_all, bias, p + jnp.arange(t), jnp.arange(p + t))
    return gated_out(o.reshape(b, t, SB_WIDTH), gate, w_out), jnp.stack([k, v], axis=2)


def dil_project(h, w_in, g_q, g_k):
    b, t, _ = h.shape
    q, k, v, gate = jnp.split(h @ w_in, [DIL_QKV_WIDTH, 2 * DIL_QKV_WIDTH, 3 * DIL_QKV_WIDTH], axis=-1)
    shp = (b, t, N_DIL_GROUPS, DIL_HEADS_PER_GROUP, HEAD_DIM)
    return rmsnorm(q.reshape(shp), g_q), rmsnorm(k.reshape(shp), g_k), v.reshape(shp), gate


def alibi_slopes():
    h = jnp.arange(1, DIL_HEADS + 1, dtype=jnp.float32)
    return jnp.exp2(-ALIBI_MAX_EXP * h / DIL_HEADS).reshape(N_DIL_GROUPS, DIL_HEADS_PER_GROUP)


def dil_group_attend(q, k_ext, v_ext, off, q_pos, dilation, window, slopes):
    n_q = q.shape[1]
    dist = jnp.arange(window // dilation + 1) * dilation
    rows = off + jnp.arange(n_q)[:, None] - dist[None, :]
    valid = (q_pos[:, None] - dist[None, :]) >= 0
    rows = jnp.clip(rows, 0, k_ext.shape[1] - 1)
    kg = k_ext[:, rows]
    vg = v_ext[:, rows]
    logits = jnp.einsum('bqhd,bqnhd->bhqn', q, kg).astype(jnp.float32) * (HEAD_DIM ** -0.5)
    logits = logits - slopes[None, :, None, None] * dist.astype(jnp.float32)
    logits = jnp.where(valid[None, None], logits, -jnp.inf)
    m = jnp.max(logits, axis=-1, keepdims=True)
    p = jnp.exp(logits - m)
    den = jnp.sum(p, axis=-1, keepdims=True)
    o = jnp.einsum('bhqn,bqnhd->bqhd', p / den, vg.astype(jnp.float32))
    return o, (m + jnp.log(den))[..., 0]


def combine_groups(outs, lses):
    w = jax.nn.softmax(jnp.stack(lses), axis=0)
    w = jnp.transpose(w, (0, 1, 3, 2))[..., None]
    return jnp.sum(w * jnp.stack(outs), axis=0)


def dil_prompt(h, w_in, g_q, g_k, w_out):
    b, s, _ = h.shape
    q, k, v, gate = dil_project(h, w_in, g_q, g_k)
    slopes = alibi_slopes()
    k_pad = [jnp.pad(k[:, :, g], ((0, 0), (win, 0), (0, 0), (0, 0))) for g, (win, _) in enumerate(DIL_GROUPS)]
    v_pad = [jnp.pad(v[:, :, g], ((0, 0), (win, 0), (0, 0), (0, 0))) for g, (win, _) in enumerate(DIL_GROUPS)]

    def block(t0):
        q_pos = t0 + jnp.arange(QUERY_BLOCK)
        outs, lses = [], []
        for g, (win, dil) in enumerate(DIL_GROUPS):
            q_blk = lax.dynamic_slice_in_dim(q[:, :, g], t0, QUERY_BLOCK, axis=1)
            k_blk = lax.dynamic_slice_in_dim(k_pad[g], t0, win + QUERY_BLOCK, axis=1)
            v_blk = lax.dynamic_slice_in_dim(v_pad[g], t0, win + QUERY_BLOCK, axis=1)
            o, l = dil_group_attend(q_blk, k_blk, v_blk, win, q_pos, dil, win, slopes[g])
            outs.append(o)
            lses.append(l)
        return combine_groups(outs, lses)

    o = lax.map(block, jnp.arange(0, s, QUERY_BLOCK))
    o = jnp.moveaxis(o, 0, 1).reshape(b, s, DIL_OUT_WIDTH)
    kv = jnp.stack([k, v], axis=3)
    windows = [kv[:, s - min(win, s):, g] for g, (win, _) in enumerate(DIL_GROUPS)]
    return gated_out(o, gate, w_out), windows


def dil_sample(h, past_windows, past_len, w_in, g_q, g_k, w_out):
    b, t, _ = h.shape
    q, k, v, gate = dil_project(h, w_in, g_q, g_k)
    slopes = alibi_slopes()
    q_pos = past_len + jnp.arange(t)
    kv_new = jnp.stack([k, v], axis=3)
    outs, lses, windows = [], [], []
    for g, (win, dil) in enumerate(DIL_GROUPS):
        buf = past_windows[g].astype(kv_new.dtype)
        ext = jnp.concatenate([buf, kv_new[:, :, g]], axis=1)
        o, l = dil_group_attend(q[:, :, g], ext[:, :, 0], ext[:, :, 1], buf.shape[1], q_pos, dil, win, slopes[g])
        outs.append(o)
        lses.append(l)
        windows.append(ext[:, ext.shape[1] - min(win, ext.shape[1]):])
    o = combine_groups(outs, lses).reshape(b, t, DIL_OUT_WIDTH)
    return gated_out(o, gate, w_out), windows


def setup_inputs(seed: int = 0) -> dict:
    key = jax.random.key(seed)
    ks = jax.random.split(key, 16)
    n_pages = PAST_LEN // PAGE_SIZE
    n_used = DEC_BATCH * n_pages
    n_phys = n_used + (n_used + 3) // 4
    f32 = jnp.float32
    x_prompt = jax.random.normal(ks[0], (BATCH, SEQ, D_MODEL), f32)
    x_sample = jax.random.normal(ks[1], (DEC_BATCH, DEC_SEQ, D_MODEL), f32)
    cache_sb = jax.random.normal(ks[2], (n_phys, N_SB_LAYERS, PAGE_SIZE, 2, SB_HEADS, HEAD_DIM), f32)
    wins = [jax.random.normal(ks[3 + g], (DEC_BATCH, N_DIL_LAYERS, min(win, PAST_LEN), 2, DIL_HEADS_PER_GROUP, HEAD_DIM), f32)
            for g, (win, _) in enumerate(DIL_GROUPS)]
    page_table = jax.random.permutation(ks[6], n_phys)[:n_used].reshape(DEC_BATCH, n_pages).astype(jnp.int32)
    norm_g = 1.0 + 0.02 * jax.random.normal(ks[7], (DEPTH, D_MODEL), f32)
    w_in_sb = jax.random.normal(ks[8], (N_SB_LAYERS, D_MODEL, 4 * SB_WIDTH), f32) * D_MODEL ** -0.5
    sb_bias = SB_BIAS_INIT + 0.5 * jax.random.normal(ks[14], (N_SB_LAYERS, SB_HEADS), f32)
    w_out_sb = jax.random.normal(ks[9], (N_SB_LAYERS, SB_WIDTH, D_MODEL), f32) * SB_WIDTH ** -0.5
    w_in_dil = jax.random.normal(ks[10], (N_DIL_LAYERS, D_MODEL, 3 * DIL_QKV_WIDTH + DIL_OUT_WIDTH), f32) * D_MODEL ** -0.5
    q_norm_dil = 1.0 + 0.02 * jax.random.normal(ks[11], (N_DIL_LAYERS, HEAD_DIM), f32)
    k_norm_dil = 1.0 + 0.02 * jax.random.normal(ks[12], (N_DIL_LAYERS, HEAD_DIM), f32)
    w_out_dil = jax.random.normal(ks[13], (N_DIL_LAYERS, DIL_OUT_WIDTH, D_MODEL), f32) * DIL_OUT_WIDTH ** -0.5
    return {'x_prompt': x_prompt, 'x_sample': x_sample, 'cache_sb': cache_sb,
            'state_win_g0': wins[0], 'state_win_g1': wins[1], 'state_win_g2': wins[2],
            'page_table': page_table, 'norm_g': norm_g, 'w_in_sb': w_in_sb, 'sb_bias': sb_bias,
            'w_out_sb': w_out_sb, 'w_in_dil': w_in_dil, 'q_norm_dil': q_norm_dil,
            'k_norm_dil': k_norm_dil, 'w_out_dil': w_out_dil}


def reference(x_prompt, x_sample, cache_sb, state_win_g0, state_win_g1, state_win_g2, page_table,
              norm_g, w_in_sb, sb_bias, w_out_sb, w_in_dil, q_norm_dil, k_norm_dil, w_out_dil):
    past_len = page_table.shape[1] * cache_sb.shape[2]
    win_states = (state_win_g0, state_win_g1, state_win_g2)
    xp, xs = x_prompt, x_sample
    sb_p, sb_s = [], []
    win_p = [[] for _ in range(N_DIL_GROUPS)]
    win_s = [[] for _ in range(N_DIL_GROUPS)]
    for layer in range(DEPTH):
        idx = layer // N_MIXERS
        hp = rmsnorm(xp, norm_g[layer])
        hs = rmsnorm(xs, norm_g[layer])
        if layer % N_MIXERS == 0:
            past = cache_sb[page_table, idx]
            past = past.reshape(past.shape[0], past_len, 2, SB_HEADS, HEAD_DIM)
            yp, rows_p = sb_prompt(hp, w_in_sb[idx], sb_bias[idx], w_out_sb[idx])
            ys, rows_s = sb_sample(hs, past, w_in_sb[idx], sb_bias[idx], w_out_sb[idx])
            sb_p.append(rows_p)
            sb_s.append(rows_s)
        else:
            yp, wp = dil_prompt(hp, w_in_dil[idx], q_norm_dil[idx], k_norm_dil[idx], w_out_dil[idx])
            ys, wsm = dil_sample(hs, [st[:, idx] for st in win_states], past_len,
                                 w_in_dil[idx], q_norm_dil[idx], k_norm_dil[idx], w_out_dil[idx])
            for g in range(N_DIL_GROUPS):
                win_p[g].append(wp[g])
                win_s[g].append(wsm[g])
        xp = xp + yp
        xs = xs + ys
    sb_rows_prompt = jnp.stack(sb_p, axis=1)
    sb_rows_sample = jnp.stack(sb_s, axis=1)
    win_g0_prompt = jnp.stack(win_p[0], axis=1)
    win_g0_sample = jnp.stack(win_s[0], axis=1)
    win_g1_prompt = jnp.stack(win_p[1], axis=1)
    win_g1_sample = jnp.stack(win_s[1], axis=1)
    win_g2_prompt = jnp.stack(win_p[2], axis=1)
    win_g2_sample = jnp.stack(win_s[2], axis=1)
    return (xp, xs, sb_rows_prompt, sb_rows_sample, win_g0_prompt, win_g0_sample,
            win_g1_prompt, win_g1_sample, win_g2_prompt, win_g2_sample)
```

```python
import functools

import jax
import jax.numpy as jnp
from jax import lax
from jax.experimental import pallas as pl
from jax.experimental.pallas import tpu as pltpu

_F32 = jnp.float32
_BF16 = jnp.bfloat16

_HEAD_DIM = 128
_DIL_GROUPS = ((128, 1), (512, 4), (2048, 16))
_DIL_HEADS_PER_GROUP = 4
_ALIBI_MAX_EXP = 8.0
_EPS = 1e-6
_N_MIXERS = 2

_LANES = 128
_TM = 512
_TN = 512
_TQ = 256
_DEC_PAGES_PER_STEP = 4
_VMEM_LIMIT = 56 * 1024 * 1024


def _dot(a, b):
  return jnp.dot(a, b, preferred_element_type=_F32)


def _dot_nt(a, b):
  return lax.dot_general(a, b, (((1,), (1,)), ((), ())), preferred_element_type=_F32)


def _silu(g):
  return g / (1.0 + jnp.exp(-g))


def _softplus_parts(z):
  sp = jnp.maximum(z, 0.0) + jnp.log1p(jnp.exp(-jnp.abs(z)))
  return -sp, z - sp


def _split_bf16(x):
  hi = x.astype(_BF16)
  lo = (x - hi.astype(_F32)).astype(_BF16)
  return hi, lo


def _rmsnorm_rows(x, g, eps):
  ms = jnp.mean(x * x, axis=-1, keepdims=True)
  return x * lax.rsqrt(ms + eps) * g


def _proj_sb_kernel(x_ref, g_ref, w_ref, q_ref, kv_ref, kb_ref, vb_ref, gate_ref, h_ref, *, ns):
  j = pl.program_id(1)

  @pl.when(j == 0)
  def _():
    h_ref[...] = _rmsnorm_rows(x_ref[...], g_ref[...], _EPS).astype(_BF16)

  acc = _dot(h_ref[...], w_ref[...])

  @pl.when(j < ns)
  def _():
    q_ref[...] = acc.astype(_BF16)

  @pl.when((j >= ns) & (j < 3 * ns))
  def _():
    kv_ref[...] = acc

  @pl.when((j >= ns) & (j < 2 * ns))
  def _():
    kb_ref[...] = acc.astype(_BF16)

  @pl.when((j >= 2 * ns) & (j < 3 * ns))
  def _():
    vb_ref[...] = acc.astype(_BF16)

  @pl.when(j >= 3 * ns)
  def _():
    gate_ref[...] = acc


def _proj_sb(x, g, w):
  m, d = x.shape
  width = w.shape[1] // 4
  tm, tn = min(_TM, m), min(_TN, width)
  ns = width // tn
  clip = lambda j, lo, n: jnp.clip(j - lo, 0, n - 1)
  return pl.pallas_call(
      functools.partial(_proj_sb_kernel, ns=ns),
      grid=(m // tm, 4 * ns),
      in_specs=[
          pl.BlockSpec((tm, d), lambda i, j: (i, 0)),
          pl.BlockSpec((1, d), lambda i, j: (0, 0)),
          pl.BlockSpec((d, tn), lambda i, j: (0, j)),
      ],
      out_specs=[
          pl.BlockSpec((tm, tn), lambda i, j: (i, clip(j, 0, ns))),
          pl.BlockSpec((tm, tn), lambda i, j: (i, clip(j, ns, 2 * ns))),
          pl.BlockSpec((tm, tn), lambda i, j: (i, clip(j, ns, ns))),
          pl.BlockSpec((tm, tn), lambda i, j: (i, clip(j, 2 * ns, ns))),
          pl.BlockSpec((tm, tn), lambda i, j: (i, clip(j, 3 * ns, ns))),
      ],
      out_shape=[
          jax.ShapeDtypeStruct((m, width), _BF16),
          jax.ShapeDtypeStruct((m, 2 * width), _F32),
          jax.ShapeDtypeStruct((m, width), _BF16),
          jax.ShapeDtypeStruct((m, width), _BF16),
          jax.ShapeDtypeStruct((m, width), _F32),
      ],
      scratch_shapes=[pltpu.VMEM((tm, d), _BF16)],
      compiler_params=pltpu.CompilerParams(
          dimension_semantics=("parallel", "arbitrary"), vmem_limit_bytes=_VMEM_LIMIT),
      name="proj_sb",
  )(x, g, w)


def _sb_attn_kernel(bias_ref, q_ref, k_ref, v_ref, gate_ref, o_ref, *, tq, scale):
  h = pl.program_id(1)
  qi = pl.program_id(2)
  q = q_ref[0]
  bias = bias_ref[h]
  row = lax.broadcasted_iota(jnp.int32, (tq, tq), 0)
  col = lax.broadcasted_iota(jnp.int32, (tq, tq), 1)
  lower = row > col
  cum = jnp.where(lower, 1.0, 0.0).astype(_BF16)

  def block(j, carry, acc, diagonal):
    start = pl.multiple_of(j * tq, tq)
    kb = k_ref[0, pl.ds(start, tq), :]
    vb = v_ref[0, pl.ds(start, tq), :]
    z = _dot_nt(q, kb) * scale + bias
    lk, ls = _softplus_parts(z)
    if diagonal:
      lk = jnp.where(lower, lk, 0.0)
    hi, lo = _split_bf16(lk)
    la = _dot(hi, cum) + _dot(lo, cum) + carry
    a = jnp.exp(ls + la)
    if diagonal:
      a = jnp.where(lower, a, 0.0)
    acc = acc + _dot(a.astype(_BF16), vb)
    carry = la[:, :1] + lk[:, :1]
    return carry, acc

  carry, acc = block(qi, jnp.zeros((tq, 1), _F32), jnp.zeros((tq, _HEAD_DIM), _F32), True)
  carry, acc = lax.fori_loop(
      0, qi, lambda it, c: block(qi - 1 - it, c[0], c[1], False), (carry, acc))
  o_ref[0] = (acc * _silu(gate_ref[0])).astype(_BF16)


def _sb_attn(bias, q, k, v, gate, n_heads):
  b, s, width = q.shape
  tq = min(_TQ, s)
  dh = _HEAD_DIM
  return pl.pallas_call(
      functools.partial(_sb_attn_kernel, tq=tq, scale=dh ** -0.5),
      grid=(b, n_heads, s // tq),
      in_specs=[
          pl.BlockSpec(memory_space=pltpu.SMEM),
          pl.BlockSpec((1, tq, dh), lambda bi, h, qi: (bi, qi, h)),
          pl.BlockSpec((1, s, dh), lambda bi, h, qi: (bi, 0, h)),
          pl.BlockSpec((1, s, dh), lambda bi, h, qi: (bi, 0, h)),
          pl.BlockSpec((1, tq, dh), lambda bi, h, qi: (bi, qi, h)),
      ],
      out_specs=pl.BlockSpec((1, tq, dh), lambda bi, h, qi: (bi, qi, h)),
      out_shape=jax.ShapeDtypeStruct((b, s, width), _BF16),
      compiler_params=pltpu.CompilerParams(
          dimension_semantics=("parallel", "parallel", "arbitrary"),
          vmem_limit_bytes=_VMEM_LIMIT),
      name="sb_attn",
  )(bias, q, k, v, gate)


def _outproj_kernel(a_ref, w_ref, x_ref, o_ref):
  o_ref[...] = x_ref[...] + _dot(a_ref[...], w_ref[...])


def _outproj(a, w, x):
  m, k = a.shape
  n = w.shape[1]
  tm = min(_TM, m)
  return pl.pallas_call(
      _outproj_kernel,
      grid=(m // tm,),
      in_specs=[
          pl.BlockSpec((tm, k), lambda i: (i, 0)),
          pl.BlockSpec((k, n), lambda i: (0, 0)),
          pl.BlockSpec((tm, n), lambda i: (i, 0)),
      ],
      out_specs=pl.BlockSpec((tm, n), lambda i: (i, 0)),
      out_shape=jax.ShapeDtypeStruct((m, n), _F32),
      compiler_params=pltpu.CompilerParams(
          dimension_semantics=("parallel",), vmem_limit_bytes=_VMEM_LIMIT),
      name="outproj",
  )(a, w, x)


def _sb_dec_kernel(pt_ref, bias_ref, qp_ref, kn_ref, vn_ref, gate_ref, *rest,
                   n_heads, n_tok, pages_per_step, page_rows, scale):
  del pt_ref
  page_refs = rest[:pages_per_step]
  o_ref, acc_ref, c_ref = rest[pages_per_step:]
  step = pl.program_id(1)
  n_pairs = n_heads // 2
  pr = 2 * n_tok
  rows = n_pairs * pr
  dh = _HEAD_DIM
  kv_stride = 2 * n_heads

  r2 = lax.broadcasted_iota(jnp.int32, (page_rows, 2 * page_rows), 0)
  c2 = lax.broadcasted_iota(jnp.int32, (page_rows, 2 * page_rows), 1)
  cum_ones = jnp.where((c2 >= page_rows) | (r2 > c2), 1.0, 0.0).astype(_BF16)
  top = lax.broadcasted_iota(jnp.int32, (pr, dh), 0) < n_tok
  bias_col = jnp.concatenate(
      [jnp.full((n_tok, page_rows), bias_ref[hh], _F32) for hh in range(n_heads)], axis=0)

  def process(k_of, v_of, mask):
    zs = []
    for p in range(n_pairs):
      qq = qp_ref[0, p]
      zs.append(jnp.where(top, _dot_nt(qq, k_of(2 * p)), _dot_nt(qq, k_of(2 * p + 1))))
    z = jnp.concatenate(zs, axis=0) * scale + bias_col
    lk, ls = _softplus_parts(z)
    if mask is not None:
      lk = jnp.where(mask, lk, 0.0)
    hi, lo = _split_bf16(lk)
    res = _dot(jnp.concatenate([hi, lo], axis=0), cum_ones)
    res = res[:rows] + res[rows:]
    carry = c_ref[...]
    a = jnp.exp(ls + res[:, :page_rows] + carry)
    if mask is not None:
      a = jnp.where(mask, a, 0.0)
    c_ref[...] = carry + res[:, page_rows:]
    ab = a.astype(_BF16)
    for p in range(n_pairs):
      ap = ab[p * pr:(p + 1) * pr]
      acc_ref[p] += jnp.where(top, _dot(ap, v_of(2 * p)), _dot(ap, v_of(2 * p + 1)))

  @pl.when(step == 0)
  def _():
    acc_ref[...] = jnp.zeros_like(acc_ref)
    c_ref[...] = jnp.zeros_like(c_ref)
    tok = lax.broadcasted_iota(jnp.int32, (rows, page_rows), 0) % n_tok
    key = lax.broadcasted_iota(jnp.int32, (rows, page_rows), 1)
    process(lambda hh: kn_ref[0, :, hh * dh:(hh + 1) * dh],
            lambda hh: vn_ref[0, :, hh * dh:(hh + 1) * dh],
            key < tok)

  for page in page_refs:
    process(
        lambda hh, page=page: page[0, 0, pl.ds(hh, page_rows, stride=kv_stride), :].astype(_BF16),
        lambda hh, page=page: page[0, 0, pl.ds(n_heads + hh, page_rows, stride=kv_stride), :]
        .astype(_BF16),
        None)

  @pl.when(step == pl.num_programs(1) - 1)
  def _():
    o_ref[0] = (acc_ref[...] * _silu(gate_ref[0])).astype(_BF16)


def _sb_dec(page_table, bias, qp, kn, vn, gate_p, cache, layer_idx, n_heads, n_tok):
  db, n_pages = page_table.shape
  page_rows = cache.shape[2] // (2 * n_heads)
  pps = _DEC_PAGES_PER_STEP
  n_pairs = n_heads // 2
  pr = 2 * n_tok
  dh = _HEAD_DIM
  width = n_heads * dh

  def page_spec(u):
    return pl.BlockSpec(
        (1, 1, cache.shape[2], dh),
        lambda bi, s, pt: (pt[bi, n_pages - 1 - (s * pps + u)], layer_idx, 0, 0))

  per_seq = lambda shape: pl.BlockSpec((1,) + shape, lambda bi, s, pt: (bi,) + (0,) * len(shape))
  return pl.pallas_call(
      functools.partial(_sb_dec_kernel, n_heads=n_heads, n_tok=n_tok, pages_per_step=pps,
                        page_rows=page_rows, scale=dh ** -0.5),
      grid_spec=pltpu.PrefetchScalarGridSpec(
          num_scalar_prefetch=1,
          grid=(db, n_pages // pps),
          in_specs=[
              pl.BlockSpec(memory_space=pltpu.SMEM),
              per_seq((n_pairs, pr, dh)),
              per_seq((page_rows, width)),
              per_seq((page_rows, width)),
              per_seq((n_pairs, pr, dh)),
          ] + [page_spec(u) for u in range(pps)],
          out_specs=per_seq((n_pairs, pr, dh)),
          scratch_shapes=[
              pltpu.VMEM((n_pairs, pr, dh), _F32),
              pltpu.VMEM((n_pairs * pr, page_rows), _F32),
          ],
      ),
      out_shape=jax.ShapeDtypeStruct((db, n_pairs, pr, dh), _BF16),
      compiler_params=pltpu.CompilerParams(
          dimension_semantics=("parallel", "arbitrary"), vmem_limit_bytes=_VMEM_LIMIT),
      name="sb_dec",
  )(page_table, bias, qp, kn, vn, gate_p, *([cache] * pps))


def _proj_dil_kernel(x_ref, g_ref, gq_ref, gk_ref, w_ref, *rest, dils, hg):
  ng = len(dils)
  q_refs = rest[:ng]
  k_refs = rest[ng:2 * ng]
  v_refs = rest[2 * ng:3 * ng]
  kvw_ref, gate_ref, h_ref, slab_ref = rest[3 * ng:]
  j = pl.program_id(1)
  dh = _HEAD_DIM
  tm = x_ref.shape[0]

  @pl.when(j == 0)
  def _():
    h_ref[...] = _rmsnorm_rows(x_ref[...], g_ref[...], _EPS).astype(_BF16)

  acc = _dot(h_ref[...], w_ref[...])

  def emit(dst_ref, d, heads):
    for c, xh in enumerate(heads):
      cols = slice(c * dh, (c + 1) * dh)
      if d == 1:
        dst_ref[0, 0, :, cols] = xh.astype(_BF16)
      else:
        slab_ref[c] = xh
        for r in range(d):
          dst_ref[0, r, :, cols] = slab_ref[c, pl.ds(r, tm // d, stride=d), :].astype(_BF16)

  def heads_of(norm_ref):
    out = []
    for c in range(hg):
      xh = acc[:, c * dh:(c + 1) * dh]
      if norm_ref is not None:
        xh = _rmsnorm_rows(xh, norm_ref[...], _EPS)
      out.append(xh)
    return out

  for g in range(ng):
    @pl.when(j == g)
    def _(g=g):
      emit(q_refs[g], dils[g], heads_of(gq_ref))

    @pl.when(j == ng + g)
    def _(g=g):
      heads = heads_of(gk_ref)
      for c, xh in enumerate(heads):
        kvw_ref[:, (2 * g * hg + c) * dh:(2 * g * hg + c + 1) * dh] = xh
      emit(k_refs[g], dils[g], heads)

    @pl.when(j == 2 * ng + g)
    def _(g=g):
      heads = heads_of(None)
      for c, xh in enumerate(heads):
        kvw_ref[:, ((2 * g + 1) * hg + c) * dh:((2 * g + 1) * hg + c + 1) * dh] = xh
      emit(v_refs[g], dils[g], heads)

  @pl.when(j == 3 * ng)
  def _():
    gate_ref[...] = acc


def _proj_dil(x, g, gq, gk, w, dils, rows_per_batch):
  m, d = x.shape
  ng = len(dils)
  hg = _DIL_HEADS_PER_GROUP
  gw = hg * _HEAD_DIM
  tm = min(_TM, rows_per_batch)
  nb = m // rows_per_batch
  tiles_per_batch = rows_per_batch // tm
  assert w.shape[1] == (3 * ng + 1) * gw and _TN == gw

  def cls_spec(dil):
    return pl.BlockSpec(
        (1, dil, tm // dil, gw),
        lambda i, j: (i // tiles_per_batch, 0, i % tiles_per_batch, 0))

  cls_shape = lambda dil: jax.ShapeDtypeStruct((nb, dil, rows_per_batch // dil, gw), _BF16)
  outs = pl.pallas_call(
      functools.partial(_proj_dil_kernel, dils=dils, hg=hg),
      grid=(m // tm, 3 * ng + 1),
      in_specs=[
          pl.BlockSpec((tm, d), lambda i, j: (i, 0)),
          pl.BlockSpec((1, d), lambda i, j: (0, 0)),
          pl.BlockSpec((1, _HEAD_DIM), lambda i, j: (0, 0)),
          pl.BlockSpec((1, _HEAD_DIM), lambda i, j: (0, 0)),
          pl.BlockSpec((d, gw), lambda i, j: (0, j)),
      ],
      out_specs=[cls_spec(dl) for dl in dils] * 3 + [
          pl.BlockSpec((tm, 2 * ng * gw), lambda i, j: (i, 0)),
          pl.BlockSpec((tm, gw), lambda i, j: (i, 0)),
      ],
      out_shape=[cls_shape(dl) for dl in dils] * 3 + [
          jax.ShapeDtypeStruct((m, 2 * ng * gw), _F32),
          jax.ShapeDtypeStruct((m, gw), _F32),
      ],
      scratch_shapes=[pltpu.VMEM((tm, d), _BF16), pltpu.VMEM((hg, tm, _HEAD_DIM), _F32)],
      compiler_params=pltpu.CompilerParams(
          dimension_semantics=("parallel", "arbitrary"), vmem_limit_bytes=_VMEM_LIMIT),
      name="proj_dil",
  )(x, g, gq, gk, w)
  return outs[:ng], outs[ng:2 * ng], outs[2 * ng:3 * ng], outs[3 * ng], outs[3 * ng + 1]


def _dil_attn_kernel(slope_ref, q_ref, kp_ref, kc_ref, vp_ref, vc_ref, o_ref, lse_ref,
                     *, dil, group, hg, blk, scale):
  mblk = pl.program_id(1)
  r = pl.program_id(2)
  dh = _HEAD_DIM
  row = lax.broadcasted_iota(jnp.int32, (blk, 2 * blk), 0)
  col = lax.broadcasted_iota(jnp.int32, (blk, 2 * blk), 1)
  delta = blk + row - col
  valid = (delta >= 0) & (delta <= blk) & ((col >= blk) | (mblk > 0))
  dist = (delta * dil).astype(_F32)
  for c in range(hg):
    cols = slice(c * dh, (c + 1) * dh)
    kk = jnp.concatenate([kp_ref[0, 0, :, cols], kc_ref[0, 0, :, cols]], axis=0)
    vv = jnp.concatenate([vp_ref[0, 0, :, cols], vc_ref[0, 0, :, cols]], axis=0)
    s = _dot_nt(q_ref[0, 0, :, cols], kk) * scale - slope_ref[group * hg + c] * dist
    s = jnp.where(valid, s, -jnp.inf)
    mx = jnp.max(s, axis=1, keepdims=True)
    p = jnp.exp(s - mx)
    den = jnp.sum(p, axis=1, keepdims=True)
    o = _dot((p / den).astype(_BF16), vv)
    lse = jnp.broadcast_to(mx + jnp.log(den), (blk, dh))
    if dil == 1:
      o_ref[0, c] = o
      lse_ref[0, c] = lse
    else:
      o_ref[0, c, pl.ds(r, blk, stride=dil), :] = o
      lse_ref[0, c, pl.ds(r, blk, stride=dil), :] = lse


def _dil_attn(slopes, q, k, v, dil, group, blk):
  b, _, sd, gw = q.shape
  hg = _DIL_HEADS_PER_GROUP
  dh = _HEAD_DIM
  s = sd * dil
  cur = pl.BlockSpec((1, 1, blk, gw), lambda bi, m, r: (bi, r, m, 0))
  prev = pl.BlockSpec((1, 1, blk, gw), lambda bi, m, r: (bi, r, jnp.maximum(m - 1, 0), 0))
  out = pl.BlockSpec((1, hg, blk * dil, dh), lambda bi, m, r: (bi, 0, m, 0))
  return pl.pallas_call(
      functools.partial(_dil_attn_kernel, dil=dil, group=group, hg=hg, blk=blk, scale=dh ** -0.5),
      grid=(b, sd // blk, dil),
      in_specs=[pl.BlockSpec(memory_space=pltpu.SMEM), cur, prev, cur, prev, cur],
      out_specs=[out, out],
      out_shape=[jax.ShapeDtypeStruct((b, hg, s, dh), _F32)] * 2,
      compiler_params=pltpu.CompilerParams(
          dimension_semantics=("parallel", "arbitrary", "arbitrary"),
          vmem_limit_bytes=_VMEM_LIMIT),
      name=f"dil_attn_g{group}",
  )(slopes, q, k, k, v, v)


def _dil_out_kernel(*refs, ng, hg):
  o_refs = refs[0:2 * ng:2]
  l_refs = refs[1:2 * ng:2]
  gate_ref, w_ref, x_ref, y_ref = refs[2 * ng:]
  dh = _HEAD_DIM
  parts = []
  for c in range(hg):
    lses = [l_refs[g][0, c] for g in range(ng)]
    mx = functools.reduce(jnp.maximum, lses)
    es = [jnp.exp(l - mx) for l in lses]
    den = functools.reduce(jnp.add, es)
    comb = functools.reduce(jnp.add, [(es[g] / den) * o_refs[g][0, c] for g in range(ng)])
    parts.append((comb * _silu(gate_ref[:, c * dh:(c + 1) * dh])).astype(_BF16))
  y_ref[...] = x_ref[...] + _dot(jnp.concatenate(parts, axis=1), w_ref[...])


def _dil_out(os_, lses, gate, w, x, rows_per_batch):
  m, n = x.shape
  ng = len(os_)
  hg = _DIL_HEADS_PER_GROUP
  dh = _HEAD_DIM
  tm = min(_TM, rows_per_batch)
  tpb = rows_per_batch // tm
  head_major = pl.BlockSpec((1, hg, tm, dh), lambda i: (i // tpb, 0, i % tpb, 0))
  args = []
  for o, l in zip(os_, lses):
    args += [o, l]
  return pl.pallas_call(
      functools.partial(_dil_out_kernel, ng=ng, hg=hg),
      grid=(m // tm,),
      in_specs=[head_major] * (2 * ng) + [
          pl.BlockSpec((tm, hg * dh), lambda i: (i, 0)),
          pl.BlockSpec((hg * dh, n), lambda i: (0, 0)),
          pl.BlockSpec((tm, n), lambda i: (i, 0)),
      ],
      out_specs=pl.BlockSpec((tm, n), lambda i: (i, 0)),
      out_shape=jax.ShapeDtypeStruct((m, n), _F32),
      compiler_params=pltpu.CompilerParams(
          dimension_semantics=("parallel",), vmem_limit_bytes=_VMEM_LIMIT),
      name="dil_out",
  )(*args, gate, w, x)


def _dil_dec_kernel(slope_ref, qp_ref, kn_ref, vn_ref, gate_ref, *rest, groups, hg, n_tok, pad_rows):
  ng = len(groups)
  st_refs = rest[:ng]
  o_ref = rest[ng]
  dh = _HEAD_DIM
  pr = 2 * n_tok
  n_pairs = hg // 2
  tok_col = lax.broadcasted_iota(jnp.int32, (pr, 1), 0) % n_tok
  top_col = lax.broadcasted_iota(jnp.int32, (pr, 1), 0) < n_tok
  top = lax.broadcasted_iota(jnp.int32, (pr, dh), 0) < n_tok
  row_stride = 2 * hg

  for p in range(n_pairs):
    outs, lses = [], []
    for g, (win, dil) in enumerate(groups):
      st = st_refs[g]
      qq = qp_ref[0, g, p]
      ha, hb = 2 * p, 2 * p + 1
      slope = jnp.where(top_col, slope_ref[g * hg + ha], slope_ref[g * hg + hb])

      def buf(off, st=st, win=win):
        return st[0, 0, pl.ds(off, win, stride=row_stride), :].astype(_BF16)

      def new(ref, hh, g=g):
        return ref[0, :, (g * hg + hh) * dh:(g * hg + hh + 1) * dh]

      brow = lax.broadcasted_iota(jnp.int32, (pr, win), 1)
      bdist = win + tok_col - brow
      bvalid = (brow >= tok_col) & ((bdist & (dil - 1)) == 0)
      sb_ = jnp.where(top_col, _dot_nt(qq, buf(ha)), _dot_nt(qq, buf(hb)))
      sb_ = sb_ * (dh ** -0.5) - slope * bdist.astype(_F32)
      sb_ = jnp.where(bvalid, sb_, -jnp.inf)
      ncol = lax.broadcasted_iota(jnp.int32, (pr, pad_rows), 1)
      ndist = tok_col - ncol
      nvalid = (ndist >= 0) & ((ndist & (dil - 1)) == 0)
      sn = jnp.where(top_col, _dot_nt(qq, new(kn_ref, ha)), _dot_nt(qq, new(kn_ref, hb)))
      sn = sn * (dh ** -0.5) - slope * ndist.astype(_F32)
      sn = jnp.where(nvalid, sn, -jnp.inf)

      mx = jnp.maximum(jnp.max(sb_, axis=1, keepdims=True), jnp.max(sn, axis=1, keepdims=True))
      pb = jnp.exp(sb_ - mx)
      pn = jnp.exp(sn - mx)
      den = jnp.sum(pb, axis=1, keepdims=True) + jnp.sum(pn, axis=1, keepdims=True)
      pb = (pb / den).astype(_BF16)
      pn = (pn / den).astype(_BF16)
      o = jnp.where(top, _dot(pb, buf(hg + ha)) + _dot(pn, new(vn_ref, ha)),
                    _dot(pb, buf(hg + hb)) + _dot(pn, new(vn_ref, hb)))
      outs.append(o)
      lses.append(mx + jnp.log(den))
    mx = functools.reduce(jnp.maximum, lses)
    es = [jnp.exp(l - mx) for l in lses]
    den = functools.reduce(jnp.add, es)
    comb = functools.reduce(jnp.add, [(es[g] / den) * outs[g] for g in range(ng)])
    o_ref[0, p] = (comb * _silu(gate_ref[0, p])).astype(_BF16)


def _dil_dec(slopes, qp, kn, vn, gate_p, states, layer_idx, groups, n_tok):
  db = qp.shape[0]
  hg = _DIL_HEADS_PER_GROUP
  dh = _HEAD_DIM
  n_pairs = hg // 2
  pr = 2 * n_tok
  pad_rows = kn.shape[1]
  per_seq = lambda shape: pl.BlockSpec((1,) + shape, lambda bi: (bi,) + (0,) * len(shape))
  st_specs = [
      pl.BlockSpec((1, 1, st.shape[2], dh), lambda bi: (bi, layer_idx, 0, 0)) for st in states]
  return pl.pallas_call(
      functools.partial(_dil_dec_kernel, groups=groups, hg=hg, n_tok=n_tok, pad_rows=pad_rows),
      grid=(db,),
      in_specs=[
          pl.BlockSpec(memory_space=pltpu.SMEM),
          per_seq(qp.shape[1:]),
          per_seq(kn.shape[1:]),
          per_seq(vn.shape[1:]),
          per_seq((n_pairs, pr, dh)),
      ] + st_specs,
      out_specs=per_seq((n_pairs, pr, dh)),
      out_shape=jax.ShapeDtypeStruct((db, n_pairs, pr, dh), _BF16),
      compiler_params=pltpu.CompilerParams(
          dimension_semantics=("parallel",), vmem_limit_bytes=_VMEM_LIMIT),
      name="dil_dec",
  )(slopes, qp, kn, vn, gate_p, *states)


def _to_pairs(a, db, n_tok, n_heads):
  a = a.reshape(db, n_tok, n_heads, _HEAD_DIM).transpose(0, 2, 1, 3)
  return a.reshape(db, n_heads // 2, 2 * n_tok, _HEAD_DIM)


def _from_pairs(a, db, n_tok, n_heads):
  a = a.reshape(db, n_heads, n_tok, _HEAD_DIM).transpose(0, 2, 1, 3)
  return a.reshape(db * n_tok, n_heads * _HEAD_DIM)


def _pad_tokens(a, db, n_tok, rows):
  a = a.reshape(db, n_tok, a.shape[-1])
  return jnp.pad(a, ((0, 0), (0, rows - n_tok), (0, 0)))


def kernel(x_prompt, x_sample, cache_sb, state_win_g0, state_win_g1, state_win_g2, page_table,
           norm_g, w_in_sb, sb_bias, w_out_sb, w_in_dil, q_norm_dil, k_norm_dil, w_out_dil):
  b, s, d_model = x_prompt.shape
  db, n_tok, _ = x_sample.shape
  depth = norm_g.shape[0]
  dh = _HEAD_DIM
  hg = _DIL_HEADS_PER_GROUP
  ng = len(_DIL_GROUPS)
  sb_heads = sb_bias.shape[1]
  page_rows = cache_sb.shape[2]
  states = (state_win_g0, state_win_g1, state_win_g2)
  dils = tuple(dl for _, dl in _DIL_GROUPS)
  for (win, dl), st in zip(_DIL_GROUPS, states):
    assert st.shape[2] == win and win // dl == _LANES and s % (dl * _LANES) == 0
  assert page_table.shape[1] * page_rows >= max(w for w, _ in _DIL_GROUPS)

  xp = x_prompt.reshape(b * s, d_model)
  xs = x_sample.reshape(db * n_tok, d_model)
  cache = cache_sb.reshape(cache_sb.shape[0], cache_sb.shape[1], page_rows * 2 * sb_heads, dh)
  states_r = [st.reshape(db, st.shape[1], st.shape[2] * 2 * hg, dh) for st in states]
  heads = jnp.arange(1, ng * hg + 1, dtype=_F32)
  slopes = jnp.exp2(-_ALIBI_MAX_EXP * heads / (ng * hg))

  sb_p, sb_s = [], []
  win_p = [[] for _ in range(ng)]
  win_s = [[] for _ in range(ng)]
  for layer in range(depth):
    idx = layer // _N_MIXERS
    g = norm_g[layer].reshape(1, d_model)
    if layer % _N_MIXERS == 0:
      w_in = w_in_sb[idx].astype(_BF16)
      w_out = w_out_sb[idx].astype(_BF16)
      bias = sb_bias[idx]
      q, kv, kb, vb, gate = _proj_sb(xp, g, w_in)
      r3 = lambda a: a.reshape(b, s, a.shape[-1])
      og = _sb_attn(bias, r3(q), r3(kb), r3(vb), r3(gate), sb_heads)
      xp = _outproj(og.reshape(b * s, -1), w_out, xp)
      sb_p.append(kv.reshape(b, s, 2, sb_heads, dh))
      q, kv, kb, vb, gate = _proj_sb(xs, g, w_in)
      og = _sb_dec(page_table, bias, _to_pairs(q, db, n_tok, sb_heads),
                   _pad_tokens(kb, db, n_tok, page_rows), _pad_tokens(vb, db, n_tok, page_rows),
                   _to_pairs(gate, db, n_tok, sb_heads), cache, idx, sb_heads, n_tok)
      xs = _outproj(_from_pairs(og, db, n_tok, sb_heads), w_out, xs)
      sb_s.append(kv.reshape(db, n_tok, 2, sb_heads, dh))
    else:
      w_in = w_in_dil[idx].astype(_BF16)
      w_out = w_out_dil[idx].astype(_BF16)
      gq = q_norm_dil[idx].reshape(1, dh)
      gk = k_norm_dil[idx].reshape(1, dh)
      qs, ks, vs, kvw, gate = _proj_dil(xp, g, gq, gk, w_in, dils, s)
      os_, lses = [], []
      for gi, (win, dl) in enumerate(_DIL_GROUPS):
        o, lse = _dil_attn(slopes, qs[gi], ks[gi], vs[gi], dl, gi, win // dl)
        os_.append(o)
        lses.append(lse)
      xp = _dil_out(os_, lses, gate, w_out, xp, s)
      kvw = kvw.reshape(b, s, ng, 2, hg, dh)
      for gi, (win, _) in enumerate(_DIL_GROUPS):
        win_p[gi].append(kvw[:, s - min(win, s):, gi])
      ones = (1,) * ng
      qs, ks, vs, kvw, gate = _proj_dil(xs, g, gq, gk, w_in, ones, db * n_tok)
      cat = lambda parts: jnp.concatenate([a.reshape(db * n_tok, hg * dh) for a in parts], axis=1)
      qp = jnp.stack([_to_pairs(a.reshape(db * n_tok, hg * dh), db, n_tok, hg) for a in qs], axis=1)
      og = _dil_dec(slopes, qp, _pad_tokens(cat(ks), db, n_tok, _LANES),
                    _pad_tokens(cat(vs), db, n_tok, _LANES), _to_pairs(gate, db, n_tok, hg),
                    states_r, idx, _DIL_GROUPS, n_tok)
      xs = _outproj(_from_pairs(og, db, n_tok, hg), w_out, xs)
      kvw = kvw.reshape(db, n_tok, ng, 2, hg, dh)
      for gi, (win, _) in enumerate(_DIL_GROUPS):
        ext = jnp.concatenate([states[gi][:, idx], kvw[:, :, gi]], axis=1)
        win_s[gi].append(ext[:, ext.shape[1] - win:])
  outs = [xp.reshape(b, s, d_model), xs.reshape(db, n_tok, d_model),
          jnp.stack(sb_p, axis=1), jnp.stack(sb_s, axis=1)]
  for gi in range(ng):
    outs += [jnp.stack(win_p[gi], axis=1), jnp.stack(win_s[gi], axis=1)]
  return tuple(outs)
```

```python
import functools

import jax
import jax.numpy as jnp
from jax import lax
from jax.experimental import pallas as pl
from jax.experimental.pallas import tpu as pltpu

_F32 = jnp.float32
_BF16 = jnp.bfloat16

_HEAD_DIM = 128
_DIL_GROUPS = ((128, 1), (512, 4), (2048, 16))
_DIL_HEADS_PER_GROUP = 4
_ALIBI_MAX_EXP = 8.0
_EPS = 1e-6
_N_MIXERS = 2
_LOG2E = 1.4426950408889634

_LANES = 128
_TM = 512
_TN = 512
_TQ = 1024
_TK = 256
_SB_HEADS_PER_STEP = 2
_DEC_PAGES_PER_STEP = 4
_VMEM_LIMIT = 56 * 1024 * 1024


def _dot(a, b):
  return jnp.dot(a, b, preferred_element_type=_F32)


def _dot_nt(a, b):
  return lax.dot_general(a, b, (((1,), (1,)), ((), ())), preferred_element_type=_F32)


def _silu(g):
  return g / (1.0 + jnp.exp(-g))


def _softplus2_parts(z2):
  sign = jnp.int32(-2 ** 31)
  neg_abs = lax.bitcast_convert_type(lax.bitcast_convert_type(z2, jnp.int32) | sign, _F32)
  sp2 = jnp.maximum(z2, 0.0) + jnp.log(1.0 + jnp.exp2(neg_abs)) * _LOG2E
  return sp2, z2 - sp2


def _split_bf16(x):
  hi = x.astype(_BF16)
  lo = (x - hi.astype(_F32)).astype(_BF16)
  return hi, lo


def _rmsnorm_rows(x, g, eps):
  ms = jnp.mean(x * x, axis=-1, keepdims=True)
  return x * lax.rsqrt(ms + eps) * g


def _proj_sb_kernel(x_ref, g_ref, w_ref, q_ref, kv_ref, kb_ref, vb_ref, gate_ref, h_ref, *, ns):
  j = pl.program_id(1)

  @pl.when(j == 0)
  def _():
    h_ref[...] = _rmsnorm_rows(x_ref[...], g_ref[...], _EPS).astype(_BF16)

  acc = _dot(h_ref[...], w_ref[...])

  @pl.when(j < ns)
  def _():
    q_ref[...] = (acc * (_HEAD_DIM ** -0.5 * _LOG2E)).astype(_BF16)

  @pl.when((j >= ns) & (j < 3 * ns))
  def _():
    kv_ref[...] = acc

  @pl.when((j >= ns) & (j < 2 * ns))
  def _():
    kb_ref[...] = acc.astype(_BF16)

  @pl.when((j >= 2 * ns) & (j < 3 * ns))
  def _():
    vb_ref[...] = acc.astype(_BF16)

  @pl.when(j >= 3 * ns)
  def _():
    gate_ref[...] = acc


def _proj_sb(x, g, w):
  m, d = x.shape
  width = w.shape[1] // 4
  tm, tn = min(_TM, m), min(_TN, width)
  ns = width // tn
  clip = lambda j, lo, n: jnp.clip(j - lo, 0, n - 1)
  return pl.pallas_call(
      functools.partial(_proj_sb_kernel, ns=ns),
      grid=(m // tm, 4 * ns),
      in_specs=[
          pl.BlockSpec((tm, d), lambda i, j: (i, 0)),
          pl.BlockSpec((1, d), lambda i, j: (0, 0)),
          pl.BlockSpec((d, tn), lambda i, j: (0, j)),
      ],
      out_specs=[
          pl.BlockSpec((tm, tn), lambda i, j: (i, clip(j, 0, ns))),
          pl.BlockSpec((tm, tn), lambda i, j: (i, clip(j, ns, 2 * ns))),
          pl.BlockSpec((tm, tn), lambda i, j: (i, clip(j, ns, ns))),
          pl.BlockSpec((tm, tn), lambda i, j: (i, clip(j, 2 * ns, ns))),
          pl.BlockSpec((tm, tn), lambda i, j: (i, clip(j, 3 * ns, ns))),
      ],
      out_shape=[
          jax.ShapeDtypeStruct((m, width), _BF16),
          jax.ShapeDtypeStruct((m, 2 * width), _F32),
          jax.ShapeDtypeStruct((m, width), _BF16),
          jax.ShapeDtypeStruct((m, width), _BF16),
          jax.ShapeDtypeStruct((m, width), _F32),
      ],
      scratch_shapes=[pltpu.VMEM((tm, d), _BF16)],
      compiler_params=pltpu.CompilerParams(
          dimension_semantics=("parallel", "arbitrary"), vmem_limit_bytes=_VMEM_LIMIT),
      name="proj_sb",
  )(x, g, w)


def _sb_attn_kernel(bias_ref, q_ref, k_ref, v_ref, gate_ref, o_ref, *, tq, tk, hp):
  hb = pl.program_id(1)
  qi = pl.program_id(2)
  dh = _HEAD_DIM
  nd = tq // tk
  row = lax.broadcasted_iota(jnp.int32, (2 * tk, tk), 0) & (tk - 1)
  col = lax.broadcasted_iota(jnp.int32, (2 * tk, tk), 1)
  cum2 = jnp.where(row > col, 1.0, 0.0).astype(_BF16)
  heads = [slice(c * dh, (c + 1) * dh) for c in range(hp)]
  bias2 = [bias_ref[hb * hp + c] * _LOG2E for c in range(hp)]

  def block(j, state, u):
    r0 = 0 if u is None else u * tk
    start = pl.multiple_of(j * tk, tk)
    out = []
    for c, cols in enumerate(heads):
      carry, acc = state[c]
      kb = k_ref[0, pl.ds(start, tk), cols]
      vb = v_ref[0, pl.ds(start, tk), cols]
      z2 = _dot_nt(q_ref[0, r0:, cols], kb) + bias2[c]
      sp2, ls2 = _softplus2_parts(z2)
      if u is not None:
        qrow = lax.broadcasted_iota(jnp.int32, (tq - r0, tk), 0)
        kcol = lax.broadcasted_iota(jnp.int32, (tq - r0, tk), 1)
        strict = kcol < qrow
        sp2 = jnp.where(strict, sp2, 0.0)
      hi, lo = _split_bf16(sp2)
      res = _dot(jnp.concatenate([hi, lo], axis=1), cum2)
      a = jnp.exp2(ls2 - res)
      if u is not None:
        a = jnp.where(strict, a, 0.0)
      pv = _dot(a.astype(_BF16), vb) * jnp.exp2(-carry[r0:])
      new_carry = carry[r0:] + res[:, :1] + sp2[:, :1]
      if r0:
        pv = jnp.concatenate([jnp.zeros((r0, dh), _F32), pv], axis=0)
        new_carry = jnp.concatenate([carry[:r0], new_carry], axis=0)
      out.append((new_carry, acc + pv))
    return tuple(out)

  state = tuple((jnp.zeros((tq, 1), _F32), jnp.zeros((tq, dh), _F32)) for _ in heads)
  for u in reversed(range(nd)):
    state = block(qi * nd + u, state, u)
  state = lax.fori_loop(0, qi * nd, lambda it, st: block(qi * nd - 1 - it, st, None), state)
  for c, cols in enumerate(heads):
    o_ref[0, :, cols] = (state[c][1] * _silu(gate_ref[0, :, cols])).astype(_BF16)


def _sb_attn(bias, q, k, v, gate, n_heads):
  b, s, width = q.shape
  tq = min(_TQ, s)
  tk = min(_TK, tq)
  hp = min(_SB_HEADS_PER_STEP, n_heads)
  w = hp * _HEAD_DIM
  return pl.pallas_call(
      functools.partial(_sb_attn_kernel, tq=tq, tk=tk, hp=hp),
      grid=(b, n_heads // hp, s // tq),
      in_specs=[
          pl.BlockSpec(memory_space=pltpu.SMEM),
          pl.BlockSpec((1, tq, w), lambda bi, h, qi: (bi, qi, h)),
          pl.BlockSpec((1, s, w), lambda bi, h, qi: (bi, 0, h)),
          pl.BlockSpec((1, s, w), lambda bi, h, qi: (bi, 0, h)),
          pl.BlockSpec((1, tq, w), lambda bi, h, qi: (bi, qi, h)),
      ],
      out_specs=pl.BlockSpec((1, tq, w), lambda bi, h, qi: (bi, qi, h)),
      out_shape=jax.ShapeDtypeStruct((b, s, width), _BF16),
      compiler_params=pltpu.CompilerParams(
          dimension_semantics=("parallel", "parallel", "arbitrary"),
          vmem_limit_bytes=_VMEM_LIMIT),
      name="sb_attn",
  )(bias, q, k, v, gate)


def _outproj_kernel(a_ref, w_ref, x_ref, o_ref):
  o_ref[...] = x_ref[...] + _dot(a_ref[...], w_ref[...])


def _outproj(a, w, x):
  m, k = a.shape
  n = w.shape[1]
  tm = min(_TM, m)
  return pl.pallas_call(
      _outproj_kernel,
      grid=(m // tm,),
      in_specs=[
          pl.BlockSpec((tm, k), lambda i: (i, 0)),
          pl.BlockSpec((k, n), lambda i: (0, 0)),
          pl.BlockSpec((tm, n), lambda i: (i, 0)),
      ],
      out_specs=pl.BlockSpec((tm, n), lambda i: (i, 0)),
      out_shape=jax.ShapeDtypeStruct((m, n), _F32),
      compiler_params=pltpu.CompilerParams(
          dimension_semantics=("parallel",), vmem_limit_bytes=_VMEM_LIMIT),
      name="outproj",
  )(a, w, x)


def _sb_dec_kernel(pt_ref, bias_ref, qp_ref, kn_ref, vn_ref, gate_ref, *rest,
                   n_heads, n_tok, pages_per_step, page_rows):
  del pt_ref
  page_refs = rest[:pages_per_step]
  o_ref, acc_ref, c_ref = rest[pages_per_step:]
  step = pl.program_id(1)
  n_pairs = n_heads // 2
  pr = 2 * n_tok
  rows = n_pairs * pr
  dh = _HEAD_DIM
  kv_stride = 2 * n_heads

  r2 = lax.broadcasted_iota(jnp.int32, (page_rows, 2 * page_rows), 0)
  c2 = lax.broadcasted_iota(jnp.int32, (page_rows, 2 * page_rows), 1)
  cum_ones = jnp.where((c2 >= page_rows) | (r2 > c2), 1.0, 0.0).astype(_BF16)
  top = lax.broadcasted_iota(jnp.int32, (pr, dh), 0) < n_tok
  bias_col = jnp.concatenate(
      [jnp.full((n_tok, page_rows), bias_ref[hh] * _LOG2E, _F32) for hh in range(n_heads)], axis=0)

  def process(k_of, v_of, mask, carry, accs):
    zs = []
    for p in range(n_pairs):
      qq = qp_ref[0, p]
      zs.append(jnp.where(top, _dot_nt(qq, k_of(2 * p)), _dot_nt(qq, k_of(2 * p + 1))))
    z2 = jnp.concatenate(zs, axis=0) + bias_col
    sp2, ls2 = _softplus2_parts(z2)
    if mask is not None:
      sp2 = jnp.where(mask, sp2, 0.0)
    hi, lo = _split_bf16(sp2)
    res = _dot(jnp.concatenate([hi, lo], axis=0), cum_ones)
    res = res[:rows] + res[rows:]
    a = jnp.exp2(ls2 - res[:, :page_rows] - carry)
    if mask is not None:
      a = jnp.where(mask, a, 0.0)
    ab = a.astype(_BF16)
    new_accs = []
    for p in range(n_pairs):
      ap = ab[p * pr:(p + 1) * pr]
      new_accs.append(
          accs[p] + jnp.where(top, _dot(ap, v_of(2 * p)), _dot(ap, v_of(2 * p + 1))))
    return carry + res[:, page_rows:], new_accs

  @pl.when(step == 0)
  def _():
    tok = lax.broadcasted_iota(jnp.int32, (rows, page_rows), 0) % n_tok
    key = lax.broadcasted_iota(jnp.int32, (rows, page_rows), 1)
    carry, accs = process(
        lambda hh: kn_ref[0, :, hh * dh:(hh + 1) * dh],
        lambda hh: vn_ref[0, :, hh * dh:(hh + 1) * dh],
        key < tok, jnp.zeros((rows, page_rows), _F32),
        [jnp.zeros((pr, dh), _F32) for _ in range(n_pairs)])
    c_ref[...] = carry
    for p in range(n_pairs):
      acc_ref[p] = accs[p]

  carry = c_ref[...]
  accs = [acc_ref[p] for p in range(n_pairs)]
  for page in page_refs:
    carry, accs = process(
        lambda hh, page=page: page[0, 0, pl.ds(hh, page_rows, stride=kv_stride), :].astype(_BF16),
        lambda hh, page=page: page[0, 0, pl.ds(n_heads + hh, page_rows, stride=kv_stride), :]
        .astype(_BF16),
        None, carry, accs)
  c_ref[...] = carry
  for p in range(n_pairs):
    acc_ref[p] = accs[p]

  @pl.when(step == pl.num_programs(1) - 1)
  def _():
    for p in range(n_pairs):
      o_ref[0, p] = (accs[p] * _silu(gate_ref[0, p])).astype(_BF16)


def _sb_dec(page_table, bias, qp, kn, vn, gate_p, cache, layer_idx, n_heads, n_tok):
  db, n_pages = page_table.shape
  page_rows = cache.shape[2] // (2 * n_heads)
  pps = _DEC_PAGES_PER_STEP
  n_pairs = n_heads // 2
  pr = 2 * n_tok
  dh = _HEAD_DIM
  width = n_heads * dh

  def page_spec(u):
    return pl.BlockSpec(
        (1, 1, cache.shape[2], dh),
        lambda bi, s, pt: (pt[bi, n_pages - 1 - (s * pps + u)], layer_idx, 0, 0))

  per_seq = lambda shape: pl.BlockSpec((1,) + shape, lambda bi, s, pt: (bi,) + (0,) * len(shape))
  return pl.pallas_call(
      functools.partial(_sb_dec_kernel, n_heads=n_heads, n_tok=n_tok, pages_per_step=pps,
                        page_rows=page_rows),
      grid_spec=pltpu.PrefetchScalarGridSpec(
          num_scalar_prefetch=1,
          grid=(db, n_pages // pps),
          in_specs=[
              pl.BlockSpec(memory_space=pltpu.SMEM),
              per_seq((n_pairs, pr, dh)),
              per_seq((page_rows, width)),
              per_seq((page_rows, width)),
              per_seq((n_pairs, pr, dh)),
          ] + [page_spec(u) for u in range(pps)],
          out_specs=per_seq((n_pairs, pr, dh)),
          scratch_shapes=[
              pltpu.VMEM((n_pairs, pr, dh), _F32),
              pltpu.VMEM((n_pairs * pr, page_rows), _F32),
          ],
      ),
      out_shape=jax.ShapeDtypeStruct((db, n_pairs, pr, dh), _BF16),
      compiler_params=pltpu.CompilerParams(
          dimension_semantics=("parallel", "arbitrary"), vmem_limit_bytes=_VMEM_LIMIT),
      name="sb_dec",
  )(page_table, bias, qp, kn, vn, gate_p, *([cache] * pps))


def _proj_dil_kernel(x_ref, g_ref, gq_ref, gk_ref, w_ref, *rest, dils, hg):
  ng = len(dils)
  q_refs = rest[:ng]
  k_refs = rest[ng:2 * ng]
  v_refs = rest[2 * ng:3 * ng]
  kvw_ref, gate_ref, h_ref, slab_ref = rest[3 * ng:]
  j = pl.program_id(1)
  dh = _HEAD_DIM
  tm = x_ref.shape[0]

  @pl.when(j == 0)
  def _():
    h_ref[...] = _rmsnorm_rows(x_ref[...], g_ref[...], _EPS).astype(_BF16)

  acc = _dot(h_ref[...], w_ref[...])

  def emit(dst_ref, d, heads):
    for c, xh in enumerate(heads):
      cols = slice(c * dh, (c + 1) * dh)
      if d == 1:
        dst_ref[0, 0, :, cols] = xh.astype(_BF16)
      else:
        slab_ref[c] = xh
        for r in range(d):
          dst_ref[0, r, :, cols] = slab_ref[c, pl.ds(r, tm // d, stride=d), :].astype(_BF16)

  def heads_of(norm_ref):
    out = []
    for c in range(hg):
      xh = acc[:, c * dh:(c + 1) * dh]
      if norm_ref is not None:
        xh = _rmsnorm_rows(xh, norm_ref[...], _EPS)
      out.append(xh)
    return out

  for g in range(ng):
    @pl.when(j == g)
    def _(g=g):
      emit(q_refs[g], dils[g], heads_of(gq_ref))

    @pl.when(j == ng + g)
    def _(g=g):
      heads = heads_of(gk_ref)
      for c, xh in enumerate(heads):
        kvw_ref[:, (2 * g * hg + c) * dh:(2 * g * hg + c + 1) * dh] = xh
      emit(k_refs[g], dils[g], heads)

    @pl.when(j == 2 * ng + g)
    def _(g=g):
      heads = heads_of(None)
      for c, xh in enumerate(heads):
        kvw_ref[:, ((2 * g + 1) * hg + c) * dh:((2 * g + 1) * hg + c + 1) * dh] = xh
      emit(v_refs[g], dils[g], heads)

  @pl.when(j == 3 * ng)
  def _():
    gate_ref[...] = acc


def _proj_dil(x, g, gq, gk, w, dils, rows_per_batch):
  m, d = x.shape
  ng = len(dils)
  hg = _DIL_HEADS_PER_GROUP
  gw = hg * _HEAD_DIM
  tm = min(_TM, rows_per_batch)
  nb = m // rows_per_batch
  tiles_per_batch = rows_per_batch // tm
  assert w.shape[1] == (3 * ng + 1) * gw and _TN == gw

  def cls_spec(dil):
    return pl.BlockSpec(
        (1, dil, tm // dil, gw),
        lambda i, j: (i // tiles_per_batch, 0, i % tiles_per_batch, 0))

  cls_shape = lambda dil: jax.ShapeDtypeStruct((nb, dil, rows_per_batch // dil, gw), _BF16)
  outs = pl.pallas_call(
      functools.partial(_proj_dil_kernel, dils=dils, hg=hg),
      grid=(m // tm, 3 * ng + 1),
      in_specs=[
          pl.BlockSpec((tm, d), lambda i, j: (i, 0)),
          pl.BlockSpec((1, d), lambda i, j: (0, 0)),
          pl.BlockSpec((1, _HEAD_DIM), lambda i, j: (0, 0)),
          pl.BlockSpec((1, _HEAD_DIM), lambda i, j: (0, 0)),
          pl.BlockSpec((d, gw), lambda i, j: (0, j)),
      ],
      out_specs=[cls_spec(dl) for dl in dils] * 3 + [
          pl.BlockSpec((tm, 2 * ng * gw), lambda i, j: (i, 0)),
          pl.BlockSpec((tm, gw), lambda i, j: (i, 0)),
      ],
      out_shape=[cls_shape(dl) for dl in dils] * 3 + [
          jax.ShapeDtypeStruct((m, 2 * ng * gw), _F32),
          jax.ShapeDtypeStruct((m, gw), _F32),
      ],
      scratch_shapes=[pltpu.VMEM((tm, d), _BF16), pltpu.VMEM((hg, tm, _HEAD_DIM), _F32)],
      compiler_params=pltpu.CompilerParams(
          dimension_semantics=("parallel", "arbitrary"), vmem_limit_bytes=_VMEM_LIMIT),
      name="proj_dil",
  )(x, g, gq, gk, w)
  return outs[:ng], outs[ng:2 * ng], outs[2 * ng:3 * ng], outs[3 * ng], outs[3 * ng + 1]


def _dil_attn_kernel(slope_ref, q_ref, kp_ref, kc_ref, vp_ref, vc_ref, o_ref, lse_ref,
                     *, dil, group, hg, blk, scale):
  mblk = pl.program_id(1)
  r = pl.program_id(2)
  dh = _HEAD_DIM
  row = lax.broadcasted_iota(jnp.int32, (blk, 2 * blk), 0)
  col = lax.broadcasted_iota(jnp.int32, (blk, 2 * blk), 1)
  delta = blk + row - col
  valid = (delta >= 0) & (delta <= blk) & ((col >= blk) | (mblk > 0))
  dist = (delta * dil).astype(_F32)
  for c in range(hg):
    cols = slice(c * dh, (c + 1) * dh)
    kk = jnp.concatenate([kp_ref[0, 0, :, cols], kc_ref[0, 0, :, cols]], axis=0)
    vv = jnp.concatenate([vp_ref[0, 0, :, cols], vc_ref[0, 0, :, cols]], axis=0)
    s = _dot_nt(q_ref[0, 0, :, cols], kk) * scale - slope_ref[group * hg + c] * dist
    s = jnp.where(valid, s, -jnp.inf)
    mx = jnp.max(s, axis=1, keepdims=True)
    p = jnp.exp(s - mx)
    den = jnp.sum(p, axis=1, keepdims=True)
    o = _dot((p / den).astype(_BF16), vv)
    lse = jnp.broadcast_to(mx + jnp.log(den), (blk, dh))
    if dil == 1:
      o_ref[0, c] = o
      lse_ref[0, c] = lse
    else:
      o_ref[0, c, pl.ds(r, blk, stride=dil), :] = o
      lse_ref[0, c, pl.ds(r, blk, stride=dil), :] = lse


def _dil_attn(slopes, q, k, v, dil, group, blk):
  b, _, sd, gw = q.shape
  hg = _DIL_HEADS_PER_GROUP
  dh = _HEAD_DIM
  s = sd * dil
  cur = pl.BlockSpec((1, 1, blk, gw), lambda bi, m, r: (bi, r, m, 0))
  prev = pl.BlockSpec((1, 1, blk, gw), lambda bi, m, r: (bi, r, jnp.maximum(m - 1, 0), 0))
  out = pl.BlockSpec((1, hg, blk * dil, dh), lambda bi, m, r: (bi, 0, m, 0))
  return pl.pallas_call(
      functools.partial(_dil_attn_kernel, dil=dil, group=group, hg=hg, blk=blk, scale=dh ** -0.5),
      grid=(b, sd // blk, dil),
      in_specs=[pl.BlockSpec(memory_space=pltpu.SMEM), cur, prev, cur, prev, cur],
      out_specs=[out, out],
      out_shape=[jax.ShapeDtypeStruct((b, hg, s, dh), _F32)] * 2,
      compiler_params=pltpu.CompilerParams(
          dimension_semantics=("parallel", "arbitrary", "arbitrary"),
          vmem_limit_bytes=_VMEM_LIMIT),
      name=f"dil_attn_g{group}",
  )(slopes, q, k, k, v, v)


def _dil_out_kernel(*refs, ng, hg):
  o_refs = refs[0:2 * ng:2]
  l_refs = refs[1:2 * ng:2]
  gate_ref, w_ref, x_ref, y_ref = refs[2 * ng:]
  dh = _HEAD_DIM
  parts = []
  for c in range(hg):
    lses = [l_refs[g][0, c] for g in range(ng)]
    mx = functools.reduce(jnp.maximum, lses)
    es = [jnp.exp(l - mx) for l in lses]
    den = functools.reduce(jnp.add, es)
    comb = functools.reduce(jnp.add, [(es[g] / den) * o_refs[g][0, c] for g in range(ng)])
    parts.append((comb * _silu(gate_ref[:, c * dh:(c + 1) * dh])).astype(_BF16))
  y_ref[...] = x_ref[...] + _dot(jnp.concatenate(parts, axis=1), w_ref[...])


def _dil_out(os_, lses, gate, w, x, rows_per_batch):
  m, n = x.shape
  ng = len(os_)
  hg = _DIL_HEADS_PER_GROUP
  dh = _HEAD_DIM
  tm = min(_TM, rows_per_batch)
  tpb = rows_per_batch // tm
  head_major = pl.BlockSpec((1, hg, tm, dh), lambda i: (i // tpb, 0, i % tpb, 0))
  args = []
  for o, l in zip(os_, lses):
    args += [o, l]
  return pl.pallas_call(
      functools.partial(_dil_out_kernel, ng=ng, hg=hg),
      grid=(m // tm,),
      in_specs=[head_major] * (2 * ng) + [
          pl.BlockSpec((tm, hg * dh), lambda i: (i, 0)),
          pl.BlockSpec((hg * dh, n), lambda i: (0, 0)),
          pl.BlockSpec((tm, n), lambda i: (i, 0)),
      ],
      out_specs=pl.BlockSpec((tm, n), lambda i: (i, 0)),
      out_shape=jax.ShapeDtypeStruct((m, n), _F32),
      compiler_params=pltpu.CompilerParams(
          dimension_semantics=("parallel",), vmem_limit_bytes=_VMEM_LIMIT),
      name="dil_out",
  )(*args, gate, w, x)


def _dil_dec_kernel(slope_ref, qp_ref, kn_ref, vn_ref, gate_ref, *rest, groups, hg, n_tok, pad_rows):
  ng = len(groups)
  new_refs = rest[:ng]
  st_refs = rest[ng:2 * ng]
  o_ref = rest[-1 - ng]
  win_refs = rest[-ng:]
  dh = _HEAD_DIM
  pr = 2 * n_tok
  n_pairs = hg // 2
  tok_col = lax.broadcasted_iota(jnp.int32, (pr, 1), 0) % n_tok
  top_col = lax.broadcasted_iota(jnp.int32, (pr, 1), 0) < n_tok
  top = lax.broadcasted_iota(jnp.int32, (pr, dh), 0) < n_tok
  row_stride = 2 * hg
  grp = 2 * n_tok
  half = n_tok * row_stride

  for g, (win, _) in enumerate(groups):
    st, dst = st_refs[g], win_refs[g]
    n_grp = win // grp
    cg = min(8, n_grp)

    def shift(m0, n_up, st=st, dst=dst):
      dst[0, 0, pl.ds(m0, cg), :half, :] = st[0, 0, pl.ds(m0, cg), half:, :]
      dst[0, 0, pl.ds(m0, n_up), half:, :] = st[0, 0, pl.ds(m0 + 1, n_up), :half, :]

    if n_grp > cg:
      def body(i, carry, shift=shift):
        shift(i * cg, cg)
        return carry
      lax.fori_loop(0, n_grp // cg - 1, body, 0)
    shift(n_grp - cg, cg - 1)
    dst[0, 0, n_grp - 1, half:, :] = new_refs[g][0]

  for p in range(n_pairs):
    outs, lses = [], []
    for g, (win, dil) in enumerate(groups):
      st = st_refs[g]
      qq = qp_ref[0, g, p]
      ha, hb = 2 * p, 2 * p + 1
      slope = jnp.where(top_col, slope_ref[g * hg + ha], slope_ref[g * hg + hb])
      n_cls = n_tok if dil % grp == 0 else grp
      n_keys = (win // grp) * n_cls

      def buf(off, st=st, n_cls=n_cls, n_keys=n_keys):
        rows = st[0, 0, :, pl.ds(off, n_cls, stride=row_stride), :]
        return rows.reshape(n_keys, dh).astype(_BF16)

      def new(ref, hh, g=g):
        return ref[0, :, (g * hg + hh) * dh:(g * hg + hh + 1) * dh]

      bcol = lax.broadcasted_iota(jnp.int32, (pr, n_keys), 1)
      brow = (bcol >> (n_cls.bit_length() - 1)) * grp + (bcol & (n_cls - 1))
      bdist = win + tok_col - brow
      bvalid = (brow >= tok_col) & ((bdist & (dil - 1)) == 0)
      sb_ = jnp.where(top_col, _dot_nt(qq, buf(ha)), _dot_nt(qq, buf(hb)))
      sb_ = sb_ * (dh ** -0.5) - slope * bdist.astype(_F32)
      sb_ = jnp.where(bvalid, sb_, -jnp.inf)
      ncol = lax.broadcasted_iota(jnp.int32, (pr, pad_rows), 1)
      ndist = tok_col - ncol
      nvalid = (ndist >= 0) & ((ndist & (dil - 1)) == 0)
      sn = jnp.where(top_col, _dot_nt(qq, new(kn_ref, ha)), _dot_nt(qq, new(kn_ref, hb)))
      sn = sn * (dh ** -0.5) - slope * ndist.astype(_F32)
      sn = jnp.where(nvalid, sn, -jnp.inf)

      mx = jnp.maximum(jnp.max(sb_, axis=1, keepdims=True), jnp.max(sn, axis=1, keepdims=True))
      pb = jnp.exp(sb_ - mx)
      pn = jnp.exp(sn - mx)
      den = jnp.sum(pb, axis=1, keepdims=True) + jnp.sum(pn, axis=1, keepdims=True)
      pb = (pb / den).astype(_BF16)
      pn = (pn / den).astype(_BF16)
      o = jnp.where(top, _dot(pb, buf(hg + ha)) + _dot(pn, new(vn_ref, ha)),
                    _dot(pb, buf(hg + hb)) + _dot(pn, new(vn_ref, hb)))
      outs.append(o)
      lses.append(mx + jnp.log(den))
    mx = functools.reduce(jnp.maximum, lses)
    es = [jnp.exp(l - mx) for l in lses]
    den = functools.reduce(jnp.add, es)
    comb = functools.reduce(jnp.add, [(es[g] / den) * outs[g] for g in range(ng)])
    o_ref[0, p] = (comb * _silu(gate_ref[0, p])).astype(_BF16)


def _dil_dec(slopes, qp, kn, vn, gate_p, new_rows, states, prev_wins, layer_idx, groups, n_tok):
  db = qp.shape[0]
  hg = _DIL_HEADS_PER_GROUP
  dh = _HEAD_DIM
  ng = len(groups)
  n_pairs = hg // 2
  pr = 2 * n_tok
  pad_rows = kn.shape[1]
  per_seq = lambda shape: pl.BlockSpec((1,) + shape, lambda bi: (bi,) + (0,) * len(shape))
  st_specs = [
      pl.BlockSpec((1, 1) + st.shape[2:], lambda bi: (bi, layer_idx, 0, 0, 0)) for st in states]
  in_specs = [
      pl.BlockSpec(memory_space=pltpu.SMEM),
      per_seq(qp.shape[1:]),
      per_seq(kn.shape[1:]),
      per_seq(vn.shape[1:]),
      per_seq((n_pairs, pr, dh)),
  ] + [per_seq(nr.shape[1:]) for nr in new_rows] + st_specs
  args = [slopes, qp, kn, vn, gate_p, *new_rows, *states]
  aliases = {}
  if prev_wins is not None:
    for g, pw in enumerate(prev_wins):
      aliases[len(args)] = 1 + g
      args.append(pw)
      in_specs.append(pl.BlockSpec(memory_space=pl.ANY))
  outs = pl.pallas_call(
      functools.partial(_dil_dec_kernel, groups=groups, hg=hg, n_tok=n_tok, pad_rows=pad_rows),
      grid=(db,),
      in_specs=in_specs,
      out_specs=[per_seq((n_pairs, pr, dh))] + st_specs,
      out_shape=[jax.ShapeDtypeStruct((db, n_pairs, pr, dh), _BF16)]
      + [jax.ShapeDtypeStruct(st.shape, _F32) for st in states],
      input_output_aliases=aliases,
      compiler_params=pltpu.CompilerParams(
          dimension_semantics=("arbitrary",), vmem_limit_bytes=_VMEM_LIMIT),
      name="dil_dec",
  )(*args)
  return outs[0], outs[1:]


def _to_pairs(a, db, n_tok, n_heads):
  a = a.reshape(db, n_tok, n_heads, _HEAD_DIM).transpose(0, 2, 1, 3)
  return a.reshape(db, n_heads // 2, 2 * n_tok, _HEAD_DIM)


def _from_pairs(a, db, n_tok, n_heads):
  a = a.reshape(db, n_heads, n_tok, _HEAD_DIM).transpose(0, 2, 1, 3)
  return a.reshape(db * n_tok, n_heads * _HEAD_DIM)


def _pad_tokens(a, db, n_tok, rows):
  a = a.reshape(db, n_tok, a.shape[-1])
  return jnp.pad(a, ((0, 0), (0, rows - n_tok), (0, 0)))


def kernel(x_prompt, x_sample, cache_sb, state_win_g0, state_win_g1, state_win_g2, page_table,
           norm_g, w_in_sb, sb_bias, w_out_sb, w_in_dil, q_norm_dil, k_norm_dil, w_out_dil):
  b, s, d_model = x_prompt.shape
  db, n_tok, _ = x_sample.shape
  depth = norm_g.shape[0]
  dh = _HEAD_DIM
  hg = _DIL_HEADS_PER_GROUP
  ng = len(_DIL_GROUPS)
  sb_heads = sb_bias.shape[1]
  page_rows = cache_sb.shape[2]
  states = (state_win_g0, state_win_g1, state_win_g2)
  dils = tuple(dl for _, dl in _DIL_GROUPS)
  for (win, dl), st in zip(_DIL_GROUPS, states):
    assert st.shape[2] == win and win // dl == _LANES and s % (dl * _LANES) == 0
  assert page_table.shape[1] * page_rows >= max(w for w, _ in _DIL_GROUPS)

  xp = x_prompt.reshape(b * s, d_model)
  xs = x_sample.reshape(db * n_tok, d_model)
  cache = cache_sb.reshape(cache_sb.shape[0], cache_sb.shape[1], page_rows * 2 * sb_heads, dh)
  grp = 2 * n_tok
  assert grp * 2 * hg == _LANES
  states_r = [st.reshape(db, st.shape[1], st.shape[2] // grp, grp * 2 * hg, dh) for st in states]
  heads = jnp.arange(1, ng * hg + 1, dtype=_F32)
  slopes = jnp.exp2(-_ALIBI_MAX_EXP * heads / (ng * hg))

  sb_p, sb_s = [], []
  win_p = [[] for _ in range(ng)]
  wins_s = None
  for layer in range(depth):
    idx = layer // _N_MIXERS
    g = norm_g[layer].reshape(1, d_model)
    if layer % _N_MIXERS == 0:
      w_in = w_in_sb[idx].astype(_BF16)
      w_out = w_out_sb[idx].astype(_BF16)
      bias = sb_bias[idx]
      q, kv, kb, vb, gate = _proj_sb(xp, g, w_in)
      r3 = lambda a: a.reshape(b, s, a.shape[-1])
      og = _sb_attn(bias, r3(q), r3(kb), r3(vb), r3(gate), sb_heads)
      xp = _outproj(og.reshape(b * s, -1), w_out, xp)
      sb_p.append(kv.reshape(b, s, 2, sb_heads, dh))
      q, kv, kb, vb, gate = _proj_sb(xs, g, w_in)
      og = _sb_dec(page_table, bias, _to_pairs(q, db, n_tok, sb_heads),
                   _pad_tokens(kb, db, n_tok, page_rows), _pad_tokens(vb, db, n_tok, page_rows),
                   _to_pairs(gate, db, n_tok, sb_heads), cache, idx, sb_heads, n_tok)
      xs = _outproj(_from_pairs(og, db, n_tok, sb_heads), w_out, xs)
      sb_s.append(kv.reshape(db, n_tok, 2, sb_heads, dh))
    else:
      w_in = w_in_dil[idx].astype(_BF16)
      w_out = w_out_dil[idx].astype(_BF16)
      gq = q_norm_dil[idx].reshape(1, dh)
      gk = k_norm_dil[idx].reshape(1, dh)
      qs, ks, vs, kvw, gate = _proj_dil(xp, g, gq, gk, w_in, dils, s)
      os_, lses = [], []
      for gi, (win, dl) in enumerate(_DIL_GROUPS):
        o, lse = _dil_attn(slopes, qs[gi], ks[gi], vs[gi], dl, gi, win // dl)
        os_.append(o)
        lses.append(lse)
      xp = _dil_out(os_, lses, gate, w_out, xp, s)
      kvw = kvw.reshape(b, s, 2 * ng * hg * dh)
      for gi, (win, _) in enumerate(_DIL_GROUPS):
        rows = kvw[:, s - min(win, s):, gi * 2 * hg * dh:(gi + 1) * 2 * hg * dh]
        win_p[gi].append(rows.reshape(b, min(win, s), 2, hg, dh))
      ones = (1,) * ng
      qs, ks, vs, kvw, gate = _proj_dil(xs, g, gq, gk, w_in, ones, db * n_tok)
      cat = lambda parts: jnp.concatenate([a.reshape(db * n_tok, hg * dh) for a in parts], axis=1)
      qp = jnp.stack([_to_pairs(a.reshape(db * n_tok, hg * dh), db, n_tok, hg) for a in qs], axis=1)
      kvw = kvw.reshape(db, n_tok, ng, 2 * hg, dh)
      new_rows = [kvw[:, :, gi].reshape(db, n_tok * 2 * hg, dh) for gi in range(ng)]
      og, wins_s = _dil_dec(slopes, qp, _pad_tokens(cat(ks), db, n_tok, _LANES),
                            _pad_tokens(cat(vs), db, n_tok, _LANES),
                            _to_pairs(gate, db, n_tok, hg), new_rows, states_r, wins_s, idx,
                            _DIL_GROUPS, n_tok)
      xs = _outproj(_from_pairs(og, db, n_tok, hg), w_out, xs)
  outs = [xp.reshape(b, s, d_model), xs.reshape(db, n_tok, d_model),
          jnp.stack(sb_p, axis=1), jnp.stack(sb_s, axis=1)]
  for gi in range(ng):
    outs += [jnp.stack(win_p[gi], axis=1), wins_s[gi].reshape(states[gi].shape)]
  return tuple(outs)
```

```python
import functools

import jax
import jax.numpy as jnp
from jax import lax
from jax.experimental import pallas as pl
from jax.experimental.pallas import tpu as pltpu

_F32 = jnp.float32
_BF16 = jnp.bfloat16

_HEAD_DIM = 128
_DIL_GROUPS = ((128, 1), (512, 4), (2048, 16))
_DIL_HEADS_PER_GROUP = 4
_ALIBI_MAX_EXP = 8.0
_EPS = 1e-6
_N_MIXERS = 2
_LOG2E = 1.4426950408889634

_LANES = 128
_TM = 512
_TM_SB = 1024
_TN = 512
_TQ = 1024
_TK = 256
_SB_HEADS_PER_STEP = 2
_DEC_PAGES_PER_STEP = 4
_VMEM_LIMIT = 56 * 1024 * 1024


def _dot(a, b):
  return jnp.dot(a, b, preferred_element_type=_F32)


def _dot_nt(a, b):
  return lax.dot_general(a, b, (((1,), (1,)), ((), ())), preferred_element_type=_F32)


def _silu(g):
  return g / (1.0 + jnp.exp(-g))


def _softplus2_parts(z2):
  sign = jnp.int32(-2 ** 31)
  neg_abs = lax.bitcast_convert_type(lax.bitcast_convert_type(z2, jnp.int32) | sign, _F32)
  sp2 = jnp.maximum(z2, 0.0) + jnp.log(1.0 + jnp.exp2(neg_abs)) * _LOG2E
  return sp2, z2 - sp2


def _split_bf16(x):
  hi = x.astype(_BF16)
  lo = (x - hi.astype(_F32)).astype(_BF16)
  return hi, lo


def _rmsnorm_rows(x, g, eps):
  ms = jnp.mean(x * x, axis=-1, keepdims=True)
  return x * lax.rsqrt(ms + eps) * g


def _proj_sb_kernel(x_ref, g_ref, w_ref, q_ref, kv_ref, kb_ref, vb_ref, gate_ref, h_ref, *, ns):
  j = pl.program_id(1)

  @pl.when(j == 0)
  def _():
    h_ref[...] = _rmsnorm_rows(x_ref[...], g_ref[...], _EPS).astype(_BF16)

  acc = _dot(h_ref[...], w_ref[...])

  @pl.when(j < ns)
  def _():
    q_ref[...] = (acc * (_HEAD_DIM ** -0.5 * _LOG2E)).astype(_BF16)

  @pl.when((j >= ns) & (j < 3 * ns))
  def _():
    kv_ref[...] = acc

  @pl.when((j >= ns) & (j < 2 * ns))
  def _():
    kb_ref[...] = acc.astype(_BF16)

  @pl.when((j >= 2 * ns) & (j < 3 * ns))
  def _():
    vb_ref[...] = acc.astype(_BF16)

  @pl.when(j >= 3 * ns)
  def _():
    gate_ref[...] = acc


def _proj_sb(x, g, w):
  m, d = x.shape
  width = w.shape[1] // 4
  tm, tn = min(_TM_SB, m), min(_TN, width)
  ns = width // tn
  clip = lambda j, lo, n: jnp.clip(j - lo, 0, n - 1)
  return pl.pallas_call(
      functools.partial(_proj_sb_kernel, ns=ns),
      grid=(m // tm, 4 * ns),
      in_specs=[
          pl.BlockSpec((tm, d), lambda i, j: (i, 0)),
          pl.BlockSpec((1, d), lambda i, j: (0, 0)),
          pl.BlockSpec((d, tn), lambda i, j: (0, j)),
      ],
      out_specs=[
          pl.BlockSpec((tm, tn), lambda i, j: (i, clip(j, 0, ns))),
          pl.BlockSpec((tm, tn), lambda i, j: (i, clip(j, ns, 2 * ns))),
          pl.BlockSpec((tm, tn), lambda i, j: (i, clip(j, ns, ns))),
          pl.BlockSpec((tm, tn), lambda i, j: (i, clip(j, 2 * ns, ns))),
          pl.BlockSpec((tm, tn), lambda i, j: (i, clip(j, 3 * ns, ns))),
      ],
      out_shape=[
          jax.ShapeDtypeStruct((m, width), _BF16),
          jax.ShapeDtypeStruct((m, 2 * width), _F32),
          jax.ShapeDtypeStruct((m, width), _BF16),
          jax.ShapeDtypeStruct((m, width), _BF16),
          jax.ShapeDtypeStruct((m, width), _F32),
      ],
      scratch_shapes=[pltpu.VMEM((tm, d), _BF16)],
      compiler_params=pltpu.CompilerParams(
          dimension_semantics=("parallel", "arbitrary"), vmem_limit_bytes=_VMEM_LIMIT),
      name="proj_sb",
  )(x, g, w)


def _sb_attn_kernel(bias_ref, q_ref, k_ref, v_ref, gate_ref, o_ref, *, tq, tk, hp):
  hb = pl.program_id(1)
  qi = pl.program_id(2)
  dh = _HEAD_DIM
  nd = tq // tk
  row = lax.broadcasted_iota(jnp.int32, (tk, tk), 0)
  col = lax.broadcasted_iota(jnp.int32, (tk, tk), 1)
  cum = jnp.where(row > col, 1.0, 0.0).astype(_BF16)
  heads = [slice(c * dh, (c + 1) * dh) for c in range(hp)]
  bias2 = [bias_ref[hb * hp + c] * _LOG2E for c in range(hp)]

  def block(j, state, u):
    r0 = 0 if u is None else u * tk
    start = pl.multiple_of(j * tk, tk)
    out = []
    for c, cols in enumerate(heads):
      carry, acc = state[c]
      kb = k_ref[0, pl.ds(start, tk), cols]
      vb = v_ref[0, pl.ds(start, tk), cols]
      z2 = _dot_nt(q_ref[0, r0:, cols], kb) + bias2[c]
      sp2, ls2 = _softplus2_parts(z2)
      if u is not None:
        qrow = lax.broadcasted_iota(jnp.int32, (tq - r0, tk), 0)
        kcol = lax.broadcasted_iota(jnp.int32, (tq - r0, tk), 1)
        strict = kcol < qrow
        sp2 = jnp.where(strict, sp2, 0.0)
      res = _dot(sp2.astype(_BF16), cum)
      a = jnp.exp2(ls2 - res)
      if u is not None:
        a = jnp.where(strict, a, 0.0)
      pv = _dot(a.astype(_BF16), vb) * jnp.exp2(-carry[r0:])
      new_carry = carry[r0:] + res[:, :1] + sp2[:, :1]
      if r0:
        pv = jnp.concatenate([jnp.zeros((r0, dh), _F32), pv], axis=0)
        new_carry = jnp.concatenate([carry[:r0], new_carry], axis=0)
      out.append((new_carry, acc + pv))
    return tuple(out)

  state = tuple((jnp.zeros((tq, 1), _F32), jnp.zeros((tq, dh), _F32)) for _ in heads)
  for u in reversed(range(nd)):
    state = block(qi * nd + u, state, u)
  state = lax.fori_loop(0, qi * nd, lambda it, st: block(qi * nd - 1 - it, st, None), state)
  for c, cols in enumerate(heads):
    o_ref[0, :, cols] = (state[c][1] * _silu(gate_ref[0, :, cols])).astype(_BF16)


def _sb_attn(bias, q, k, v, gate, n_heads):
  b, s, width = q.shape
  tq = min(_TQ, s)
  tk = min(_TK, tq)
  hp = min(_SB_HEADS_PER_STEP, n_heads)
  w = hp * _HEAD_DIM
  return pl.pallas_call(
      functools.partial(_sb_attn_kernel, tq=tq, tk=tk, hp=hp),
      grid=(b, n_heads // hp, s // tq),
      in_specs=[
          pl.BlockSpec(memory_space=pltpu.SMEM),
          pl.BlockSpec((1, tq, w), lambda bi, h, qi: (bi, qi, h)),
          pl.BlockSpec((1, s, w), lambda bi, h, qi: (bi, 0, h)),
          pl.BlockSpec((1, s, w), lambda bi, h, qi: (bi, 0, h)),
          pl.BlockSpec((1, tq, w), lambda bi, h, qi: (bi, qi, h)),
      ],
      out_specs=pl.BlockSpec((1, tq, w), lambda bi, h, qi: (bi, qi, h)),
      out_shape=jax.ShapeDtypeStruct((b, s, width), _BF16),
      compiler_params=pltpu.CompilerParams(
          dimension_semantics=("parallel", "parallel", "arbitrary"),
          vmem_limit_bytes=_VMEM_LIMIT),
      name="sb_attn",
  )(bias, q, k, v, gate)


def _outproj_kernel(a_ref, w_ref, x_ref, o_ref):
  o_ref[...] = x_ref[...] + _dot(a_ref[...], w_ref[...])


def _outproj(a, w, x):
  m, k = a.shape
  n = w.shape[1]
  tm = min(_TM, m)
  return pl.pallas_call(
      _outproj_kernel,
      grid=(m // tm,),
      in_specs=[
          pl.BlockSpec((tm, k), lambda i: (i, 0)),
          pl.BlockSpec((k, n), lambda i: (0, 0)),
          pl.BlockSpec((tm, n), lambda i: (i, 0)),
      ],
      out_specs=pl.BlockSpec((tm, n), lambda i: (i, 0)),
      out_shape=jax.ShapeDtypeStruct((m, n), _F32),
      compiler_params=pltpu.CompilerParams(
          dimension_semantics=("parallel",), vmem_limit_bytes=_VMEM_LIMIT),
      name="outproj",
  )(a, w, x)


def _sb_dec_kernel(pt_ref, bias_ref, qp_ref, kn_ref, vn_ref, gate_ref, cache_ref, o_ref,
                   acc_ref, c_ref, kv_buf, sem, *, n_heads, n_tok, pages_per_step, page_rows,
                   layer_idx):
  seq = pl.program_id(0)
  step = pl.program_id(1)
  n_steps = pl.num_programs(1)
  n_pages = n_steps * pages_per_step
  n_pairs = n_heads // 2
  pr = 2 * n_tok
  rows = n_pairs * pr
  dh = _HEAD_DIM
  n_kv = 2 * n_heads

  def page_copies(w, slot):
    b_w = w // n_steps
    s_w = w % n_steps
    out = []
    for u in range(pages_per_step):
      page = pt_ref[b_w, n_pages - (s_w + 1) * pages_per_step + u]
      for r in range(n_kv):
        out.append(pltpu.make_async_copy(
            cache_ref.at[page, layer_idx, :, r, :],
            kv_buf.at[slot, r, pl.ds(u * page_rows, page_rows), :], sem.at[slot]))
    return out

  w = seq * n_steps + step
  slot = w % 2

  @pl.when(w == 0)
  def _():
    for cp in page_copies(w, slot):
      cp.start()

  @pl.when(w + 1 < pl.num_programs(0) * n_steps)
  def _():
    for cp in page_copies(w + 1, 1 - slot):
      cp.start()

  r2 = lax.broadcasted_iota(jnp.int32, (page_rows, 2 * page_rows), 0)
  c2 = lax.broadcasted_iota(jnp.int32, (page_rows, 2 * page_rows), 1)
  cum_ones = jnp.where((c2 >= page_rows) | (r2 > c2), 1.0, 0.0).astype(_BF16)
  top_o = lax.broadcasted_iota(jnp.int32, (pr, dh), 0) < n_tok

  def process(k_of, v_of, nb, mask, carry, accs):
    nk = nb * page_rows
    top = lax.broadcasted_iota(jnp.int32, (pr, nk), 0) < n_tok
    bias_col = jnp.concatenate(
        [jnp.full((n_tok, nk), bias_ref[hh] * _LOG2E, _F32) for hh in range(n_heads)], axis=0)
    zs = []
    for p in range(n_pairs):
      qq = qp_ref[0, p]
      zs.append(jnp.where(top, _dot_nt(qq, k_of(2 * p)), _dot_nt(qq, k_of(2 * p + 1))))
    z2 = jnp.concatenate(zs, axis=0) + bias_col
    sp2, ls2 = _softplus2_parts(z2)
    if mask is not None:
      sp2 = jnp.where(mask, sp2, 0.0)
    hi, lo = _split_bf16(sp2)
    sub = lambda x: [x[:, b * page_rows:(b + 1) * page_rows] for b in range(nb)]
    res = _dot(jnp.concatenate(sub(hi) + sub(lo), axis=0), cum_ones)
    res = res[:nb * rows] + res[nb * rows:]
    tots = [None] * nb
    for b in reversed(range(nb)):
      blk = res[b * rows:(b + 1) * rows]
      tots[b] = blk[:, :page_rows] + carry
      carry = carry + blk[:, page_rows:]
    a = jnp.exp2(ls2 - jnp.concatenate(tots, axis=1))
    if mask is not None:
      a = jnp.where(mask, a, 0.0)
    ab = a.astype(_BF16)
    new_accs = []
    for p in range(n_pairs):
      ap = ab[p * pr:(p + 1) * pr]
      new_accs.append(
          accs[p] + jnp.where(top_o, _dot(ap, v_of(2 * p)), _dot(ap, v_of(2 * p + 1))))
    return carry, new_accs

  @pl.when(step == 0)
  def _():
    tok = lax.broadcasted_iota(jnp.int32, (rows, page_rows), 0) % n_tok
    key = lax.broadcasted_iota(jnp.int32, (rows, page_rows), 1)
    carry, accs = process(
        lambda hh: kn_ref[0, :, hh * dh:(hh + 1) * dh],
        lambda hh: vn_ref[0, :, hh * dh:(hh + 1) * dh],
        1, key < tok, jnp.zeros((rows, page_rows), _F32),
        [jnp.zeros((pr, dh), _F32) for _ in range(n_pairs)])
    c_ref[...] = carry
    for p in range(n_pairs):
      acc_ref[p] = accs[p]

  for cp in page_copies(w, slot):
    cp.wait()

  carry, accs = process(
      lambda hh: kv_buf[slot, hh].astype(_BF16),
      lambda hh: kv_buf[slot, n_heads + hh].astype(_BF16),
      pages_per_step, None, c_ref[...], [acc_ref[p] for p in range(n_pairs)])
  c_ref[...] = carry
  for p in range(n_pairs):
    acc_ref[p] = accs[p]

  @pl.when(step == pl.num_programs(1) - 1)
  def _():
    for p in range(n_pairs):
      o_ref[0, p] = (accs[p] * _silu(gate_ref[0, p])).astype(_BF16)


def _sb_dec(page_table, bias, qp, kn, vn, gate_p, cache, layer_idx, n_heads, n_tok):
  db, n_pages = page_table.shape
  page_rows = cache.shape[2]
  pps = _DEC_PAGES_PER_STEP
  n_pairs = n_heads // 2
  pr = 2 * n_tok
  dh = _HEAD_DIM
  width = n_heads * dh
  per_seq = lambda shape: pl.BlockSpec((1,) + shape, lambda bi, s, pt: (bi,) + (0,) * len(shape))
  return pl.pallas_call(
      functools.partial(_sb_dec_kernel, n_heads=n_heads, n_tok=n_tok, pages_per_step=pps,
                        page_rows=page_rows, layer_idx=layer_idx),
      grid_spec=pltpu.PrefetchScalarGridSpec(
          num_scalar_prefetch=1,
          grid=(db, n_pages // pps),
          in_specs=[
              pl.BlockSpec(memory_space=pltpu.SMEM),
              per_seq((n_pairs, pr, dh)),
              per_seq((page_rows, width)),
              per_seq((page_rows, width)),
              per_seq((n_pairs, pr, dh)),
              pl.BlockSpec(memory_space=pl.ANY),
          ],
          out_specs=per_seq((n_pairs, pr, dh)),
          scratch_shapes=[
              pltpu.VMEM((n_pairs, pr, dh), _F32),
              pltpu.VMEM((n_pairs * pr, page_rows), _F32),
              pltpu.VMEM((2, 2 * n_heads, pps * page_rows, dh), _F32),
              pltpu.SemaphoreType.DMA((2,)),
          ],
      ),
      out_shape=jax.ShapeDtypeStruct((db, n_pairs, pr, dh), _BF16),
      compiler_params=pltpu.CompilerParams(
          dimension_semantics=("arbitrary", "arbitrary"), vmem_limit_bytes=_VMEM_LIMIT),
      name="sb_dec",
  )(page_table, bias, qp, kn, vn, gate_p, cache)


def _proj_dil_kernel(x_ref, g_ref, gq_ref, gk_ref, w_ref, *rest, dils, hg):
  ng = len(dils)
  q_refs = rest[:ng]
  k_refs = rest[ng:2 * ng]
  v_refs = rest[2 * ng:3 * ng]
  kvw_ref, gate_ref, h_ref, slab_ref = rest[3 * ng:]
  j = pl.program_id(1)
  dh = _HEAD_DIM
  tm = x_ref.shape[0]

  @pl.when(j == 0)
  def _():
    h_ref[...] = _rmsnorm_rows(x_ref[...], g_ref[...], _EPS).astype(_BF16)

  acc = _dot(h_ref[...], w_ref[...])

  def emit(dst_ref, d, heads):
    for c, xh in enumerate(heads):
      cols = slice(c * dh, (c + 1) * dh)
      if d == 1:
        dst_ref[0, 0, :, cols] = xh.astype(_BF16)
      else:
        slab_ref[c] = xh
        for r in range(d):
          dst_ref[0, r, :, cols] = slab_ref[c, pl.ds(r, tm // d, stride=d), :].astype(_BF16)

  def heads_of(norm_ref):
    out = []
    for c in range(hg):
      xh = acc[:, c * dh:(c + 1) * dh]
      if norm_ref is not None:
        xh = _rmsnorm_rows(xh, norm_ref[...], _EPS)
      out.append(xh)
    return out

  for g in range(ng):
    @pl.when(j == g)
    def _(g=g):
      emit(q_refs[g], dils[g], heads_of(gq_ref))

    @pl.when(j == ng + g)
    def _(g=g):
      heads = heads_of(gk_ref)
      for c, xh in enumerate(heads):
        kvw_ref[:, (2 * g * hg + c) * dh:(2 * g * hg + c + 1) * dh] = xh
      emit(k_refs[g], dils[g], heads)

    @pl.when(j == 2 * ng + g)
    def _(g=g):
      heads = heads_of(None)
      for c, xh in enumerate(heads):
        kvw_ref[:, ((2 * g + 1) * hg + c) * dh:((2 * g + 1) * hg + c + 1) * dh] = xh
      emit(v_refs[g], dils[g], heads)

  @pl.when(j == 3 * ng)
  def _():
    gate_ref[...] = acc


def _proj_dil(x, g, gq, gk, w, dils, rows_per_batch):
  m, d = x.shape
  ng = len(dils)
  hg = _DIL_HEADS_PER_GROUP
  gw = hg * _HEAD_DIM
  tm = min(_TM, rows_per_batch)
  nb = m // rows_per_batch
  tiles_per_batch = rows_per_batch // tm
  assert w.shape[1] == (3 * ng + 1) * gw and _TN == gw

  def cls_spec(dil):
    return pl.BlockSpec(
        (1, dil, tm // dil, gw),
        lambda i, j: (i // tiles_per_batch, 0, i % tiles_per_batch, 0))

  cls_shape = lambda dil: jax.ShapeDtypeStruct((nb, dil, rows_per_batch // dil, gw), _BF16)
  outs = pl.pallas_call(
      functools.partial(_proj_dil_kernel, dils=dils, hg=hg),
      grid=(m // tm, 3 * ng + 1),
      in_specs=[
          pl.BlockSpec((tm, d), lambda i, j: (i, 0)),
          pl.BlockSpec((1, d), lambda i, j: (0, 0)),
          pl.BlockSpec((1, _HEAD_DIM), lambda i, j: (0, 0)),
          pl.BlockSpec((1, _HEAD_DIM), lambda i, j: (0, 0)),
          pl.BlockSpec((d, gw), lambda i, j: (0, j)),
      ],
      out_specs=[cls_spec(dl) for dl in dils] * 3 + [
          pl.BlockSpec((tm, 2 * ng * gw), lambda i, j: (i, 0)),
          pl.BlockSpec((tm, gw), lambda i, j: (i, 0)),
      ],
      out_shape=[cls_shape(dl) for dl in dils] * 3 + [
          jax.ShapeDtypeStruct((m, 2 * ng * gw), _F32),
          jax.ShapeDtypeStruct((m, gw), _F32),
      ],
      scratch_shapes=[pltpu.VMEM((tm, d), _BF16), pltpu.VMEM((hg, tm, _HEAD_DIM), _F32)],
      compiler_params=pltpu.CompilerParams(
          dimension_semantics=("parallel", "arbitrary"), vmem_limit_bytes=_VMEM_LIMIT),
      name="proj_dil",
  )(x, g, gq, gk, w)
  return outs[:ng], outs[ng:2 * ng], outs[2 * ng:3 * ng], outs[3 * ng], outs[3 * ng + 1]


def _dil_attn_kernel(slope_ref, q_ref, kp_ref, kc_ref, vp_ref, vc_ref, o_ref, lse_ref,
                     *, dil, group, hg, blk, scale):
  mblk = pl.program_id(1)
  r = pl.program_id(2)
  dh = _HEAD_DIM
  row = lax.broadcasted_iota(jnp.int32, (blk, 2 * blk), 0)
  col = lax.broadcasted_iota(jnp.int32, (blk, 2 * blk), 1)
  delta = blk + row - col
  valid = (delta >= 0) & (delta <= blk) & ((col >= blk) | (mblk > 0))
  dist = (delta * dil).astype(_F32)
  for c in range(hg):
    cols = slice(c * dh, (c + 1) * dh)
    kk = jnp.concatenate([kp_ref[0, 0, :, cols], kc_ref[0, 0, :, cols]], axis=0)
    vv = jnp.concatenate([vp_ref[0, 0, :, cols], vc_ref[0, 0, :, cols]], axis=0)
    s = _dot_nt(q_ref[0, 0, :, cols], kk) * scale - slope_ref[group * hg + c] * dist
    s = jnp.where(valid, s, -jnp.inf)
    mx = jnp.max(s, axis=1, keepdims=True)
    p = jnp.exp(s - mx)
    den = jnp.sum(p, axis=1, keepdims=True)
    o = _dot((p / den).astype(_BF16), vv)
    lse = jnp.broadcast_to(mx + jnp.log(den), (blk, dh))
    if dil == 1:
      o_ref[0, c] = o
      lse_ref[0, c] = lse
    else:
      o_ref[0, c, pl.ds(r, blk, stride=dil), :] = o
      lse_ref[0, c, pl.ds(r, blk, stride=dil), :] = lse


def _dil_attn(slopes, q, k, v, dil, group, blk):
  b, _, sd, gw = q.shape
  hg = _DIL_HEADS_PER_GROUP
  dh = _HEAD_DIM
  s = sd * dil
  cur = pl.BlockSpec((1, 1, blk, gw), lambda bi, m, r: (bi, r, m, 0))
  prev = pl.BlockSpec((1, 1, blk, gw), lambda bi, m, r: (bi, r, jnp.maximum(m - 1, 0), 0))
  out = pl.BlockSpec((1, hg, blk * dil, dh), lambda bi, m, r: (bi, 0, m, 0))
  return pl.pallas_call(
      functools.partial(_dil_attn_kernel, dil=dil, group=group, hg=hg, blk=blk, scale=dh ** -0.5),
      grid=(b, sd // blk, dil),
      in_specs=[pl.BlockSpec(memory_space=pltpu.SMEM), cur, prev, cur, prev, cur],
      out_specs=[out, out],
      out_shape=[jax.ShapeDtypeStruct((b, hg, s, dh), _F32)] * 2,
      compiler_params=pltpu.CompilerParams(
          dimension_semantics=("parallel", "arbitrary", "arbitrary"),
          vmem_limit_bytes=_VMEM_LIMIT),
      name=f"dil_attn_g{group}",
  )(slopes, q, k, k, v, v)


def _dil_out_kernel(*refs, ng, hg):
  o_refs = refs[0:2 * ng:2]
  l_refs = refs[1:2 * ng:2]
  gate_ref, w_ref, x_ref, y_ref = refs[2 * ng:]
  dh = _HEAD_DIM
  parts = []
  for c in range(hg):
    lses = [l_refs[g][0, c] for g in range(ng)]
    mx = functools.reduce(jnp.maximum, lses)
    es = [jnp.exp(l - mx) for l in lses]
    den = functools.reduce(jnp.add, es)
    comb = functools.reduce(jnp.add, [(es[g] / den) * o_refs[g][0, c] for g in range(ng)])
    parts.append((comb * _silu(gate_ref[:, c * dh:(c + 1) * dh])).astype(_BF16))
  y_ref[...] = x_ref[...] + _dot(jnp.concatenate(parts, axis=1), w_ref[...])


def _dil_out(os_, lses, gate, w, x, rows_per_batch):
  m, n = x.shape
  ng = len(os_)
  hg = _DIL_HEADS_PER_GROUP
  dh = _HEAD_DIM
  tm = min(_TM, rows_per_batch)
  tpb = rows_per_batch // tm
  head_major = pl.BlockSpec((1, hg, tm, dh), lambda i: (i // tpb, 0, i % tpb, 0))
  args = []
  for o, l in zip(os_, lses):
    args += [o, l]
  return pl.pallas_call(
      functools.partial(_dil_out_kernel, ng=ng, hg=hg),
      grid=(m // tm,),
      in_specs=[head_major] * (2 * ng) + [
          pl.BlockSpec((tm, hg * dh), lambda i: (i, 0)),
          pl.BlockSpec((hg * dh, n), lambda i: (0, 0)),
          pl.BlockSpec((tm, n), lambda i: (i, 0)),
      ],
      out_specs=pl.BlockSpec((tm, n), lambda i: (i, 0)),
      out_shape=jax.ShapeDtypeStruct((m, n), _F32),
      compiler_params=pltpu.CompilerParams(
          dimension_semantics=("parallel",), vmem_limit_bytes=_VMEM_LIMIT),
      name="dil_out",
  )(*args, gate, w, x)


def _dil_dec_kernel(slope_ref, qp_ref, kn_ref, vn_ref, gate_ref, *rest, groups, hg, n_tok, pad_rows):
  ng = len(groups)
  new_refs = rest[:ng]
  st_refs = rest[ng:2 * ng]
  o_ref = rest[-1 - ng]
  win_refs = rest[-ng:]
  dh = _HEAD_DIM
  pr = 2 * n_tok
  n_pairs = hg // 2
  tok_col = lax.broadcasted_iota(jnp.int32, (pr, 1), 0) % n_tok
  top_col = lax.broadcasted_iota(jnp.int32, (pr, 1), 0) < n_tok
  top = lax.broadcasted_iota(jnp.int32, (pr, dh), 0) < n_tok
  row_stride = 2 * hg
  grp = 2 * n_tok
  half = n_tok * row_stride

  for g, (win, _) in enumerate(groups):
    st, dst = st_refs[g], win_refs[g]
    n_grp = win // grp
    cg = min(8, n_grp)

    def shift(m0, n_up, st=st, dst=dst):
      dst[0, 0, pl.ds(m0, cg), :half, :] = st[0, 0, pl.ds(m0, cg), half:, :]
      dst[0, 0, pl.ds(m0, n_up), half:, :] = st[0, 0, pl.ds(m0 + 1, n_up), :half, :]

    if n_grp > cg:
      def body(i, carry, shift=shift):
        shift(i * cg, cg)
        return carry
      lax.fori_loop(0, n_grp // cg - 1, body, 0)
    shift(n_grp - cg, cg - 1)
    dst[0, 0, n_grp - 1, half:, :] = new_refs[g][0]

  for p in range(n_pairs):
    outs, lses = [], []
    for g, (win, dil) in enumerate(groups):
      st = st_refs[g]
      qq = qp_ref[0, g, p]
      ha, hb = 2 * p, 2 * p + 1
      slope = jnp.where(top_col, slope_ref[g * hg + ha], slope_ref[g * hg + hb])
      n_cls = n_tok if dil % grp == 0 else grp
      n_keys = (win // grp) * n_cls

      def buf(off, st=st, n_cls=n_cls, n_keys=n_keys):
        rows = st[0, 0, :, pl.ds(off, n_cls, stride=row_stride), :]
        return rows.reshape(n_keys, dh).astype(_BF16)

      def new(ref, hh, g=g):
        return ref[0, :, (g * hg + hh) * dh:(g * hg + hh + 1) * dh]

      bcol = lax.broadcasted_iota(jnp.int32, (pr, n_keys), 1)
      brow = (bcol >> (n_cls.bit_length() - 1)) * grp + (bcol & (n_cls - 1))
      bdist = win + tok_col - brow
      bvalid = (brow >= tok_col) & ((bdist & (dil - 1)) == 0)
      sb_ = jnp.where(top_col, _dot_nt(qq, buf(ha)), _dot_nt(qq, buf(hb)))
      sb_ = sb_ * (dh ** -0.5) - slope * bdist.astype(_F32)
      sb_ = jnp.where(bvalid, sb_, -jnp.inf)
      ncol = lax.broadcasted_iota(jnp.int32, (pr, pad_rows), 1)
      ndist = tok_col - ncol
      nvalid = (ndist >= 0) & ((ndist & (dil - 1)) == 0)
      sn = jnp.where(top_col, _dot_nt(qq, new(kn_ref, ha)), _dot_nt(qq, new(kn_ref, hb)))
      sn = sn * (dh ** -0.5) - slope * ndist.astype(_F32)
      sn = jnp.where(nvalid, sn, -jnp.inf)

      mx = jnp.maximum(jnp.max(sb_, axis=1, keepdims=True), jnp.max(sn, axis=1, keepdims=True))
      pb = jnp.exp(sb_ - mx)
      pn = jnp.exp(sn - mx)
      den = jnp.sum(pb, axis=1, keepdims=True) + jnp.sum(pn, axis=1, keepdims=True)
      pb = (pb / den).astype(_BF16)
      pn = (pn / den).astype(_BF16)
      o = jnp.where(top, _dot(pb, buf(hg + ha)) + _dot(pn, new(vn_ref, ha)),
                    _dot(pb, buf(hg + hb)) + _dot(pn, new(vn_ref, hb)))
      outs.append(o)
      lses.append(mx + jnp.log(den))
    mx = functools.reduce(jnp.maximum, lses)
    es = [jnp.exp(l - mx) for l in lses]
    den = functools.reduce(jnp.add, es)
    comb = functools.reduce(jnp.add, [(es[g] / den) * outs[g] for g in range(ng)])
    o_ref[0, p] = (comb * _silu(gate_ref[0, p])).astype(_BF16)


def _dil_dec(slopes, qp, kn, vn, gate_p, new_rows, states, prev_wins, layer_idx, groups, n_tok):
  db = qp.shape[0]
  hg = _DIL_HEADS_PER_GROUP
  dh = _HEAD_DIM
  ng = len(groups)
  n_pairs = hg // 2
  pr = 2 * n_tok
  pad_rows = kn.shape[1]
  per_seq = lambda shape: pl.BlockSpec((1,) + shape, lambda bi: (bi,) + (0,) * len(shape))
  st_specs = [
      pl.BlockSpec((1, 1) + st.shape[2:], lambda bi: (bi, layer_idx, 0, 0, 0)) for st in states]
  in_specs = [
      pl.BlockSpec(memory_space=pltpu.SMEM),
      per_seq(qp.shape[1:]),
      per_seq(kn.shape[1:]),
      per_seq(vn.shape[1:]),
      per_seq((n_pairs, pr, dh)),
  ] + [per_seq(nr.shape[1:]) for nr in new_rows] + st_specs
  args = [slopes, qp, kn, vn, gate_p, *new_rows, *states]
  aliases = {}
  if prev_wins is not None:
    for g, pw in enumerate(prev_wins):
      aliases[len(args)] = 1 + g
      args.append(pw)
      in_specs.append(pl.BlockSpec(memory_space=pl.ANY))
  outs = pl.pallas_call(
      functools.partial(_dil_dec_kernel, groups=groups, hg=hg, n_tok=n_tok, pad_rows=pad_rows),
      grid=(db,),
      in_specs=in_specs,
      out_specs=[per_seq((n_pairs, pr, dh))] + st_specs,
      out_shape=[jax.ShapeDtypeStruct((db, n_pairs, pr, dh), _BF16)]
      + [jax.ShapeDtypeStruct(st.shape, _F32) for st in states],
      input_output_aliases=aliases,
      compiler_params=pltpu.CompilerParams(
          dimension_semantics=("arbitrary",), vmem_limit_bytes=_VMEM_LIMIT),
      name="dil_dec",
  )(*args)
  return outs[0], outs[1:]


def _to_pairs(a, db, n_tok, n_heads):
  a = a.reshape(db, n_tok, n_heads, _HEAD_DIM).transpose(0, 2, 1, 3)
  return a.reshape(db, n_heads // 2, 2 * n_tok, _HEAD_DIM)


def _from_pairs(a, db, n_tok, n_heads):
  a = a.reshape(db, n_heads, n_tok, _HEAD_DIM).transpose(0, 2, 1, 3)
  return a.reshape(db * n_tok, n_heads * _HEAD_DIM)


def _pad_tokens(a, db, n_tok, rows):
  a = a.reshape(db, n_tok, a.shape[-1])
  return jnp.pad(a, ((0, 0), (0, rows - n_tok), (0, 0)))


def kernel(x_prompt, x_sample, cache_sb, state_win_g0, state_win_g1, state_win_g2, page_table,
           norm_g, w_in_sb, sb_bias, w_out_sb, w_in_dil, q_norm_dil, k_norm_dil, w_out_dil):
  b, s, d_model = x_prompt.shape
  db, n_tok, _ = x_sample.shape
  depth = norm_g.shape[0]
  dh = _HEAD_DIM
  hg = _DIL_HEADS_PER_GROUP
  ng = len(_DIL_GROUPS)
  sb_heads = sb_bias.shape[1]
  page_rows = cache_sb.shape[2]
  states = (state_win_g0, state_win_g1, state_win_g2)
  dils = tuple(dl for _, dl in _DIL_GROUPS)
  for (win, dl), st in zip(_DIL_GROUPS, states):
    assert st.shape[2] == win and win // dl == _LANES and s % (dl * _LANES) == 0
  assert page_table.shape[1] * page_rows >= max(w for w, _ in _DIL_GROUPS)

  xp = x_prompt.reshape(b * s, d_model)
  xs = x_sample.reshape(db * n_tok, d_model)
  cache = cache_sb.reshape(cache_sb.shape[0], cache_sb.shape[1], page_rows, 2 * sb_heads, dh)
  grp = 2 * n_tok
  assert grp * 2 * hg == _LANES
  states_r = [st.reshape(db, st.shape[1], st.shape[2] // grp, grp * 2 * hg, dh) for st in states]
  heads = jnp.arange(1, ng * hg + 1, dtype=_F32)
  slopes = jnp.exp2(-_ALIBI_MAX_EXP * heads / (ng * hg))

  sb_p, sb_s = [], []
  win_p = [[] for _ in range(ng)]
  wins_s = None
  for layer in range(depth):
    idx = layer // _N_MIXERS
    g = norm_g[layer].reshape(1, d_model)
    if layer % _N_MIXERS == 0:
      w_in = w_in_sb[idx].astype(_BF16)
      w_out = w_out_sb[idx].astype(_BF16)
      bias = sb_bias[idx]
      q, kv, kb, vb, gate = _proj_sb(xp, g, w_in)
      r3 = lambda a: a.reshape(b, s, a.shape[-1])
      og = _sb_attn(bias, r3(q), r3(kb), r3(vb), r3(gate), sb_heads)
      xp = _outproj(og.reshape(b * s, -1), w_out, xp)
      sb_p.append(kv.reshape(b, s, 2, sb_heads, dh))
      q, kv, kb, vb, gate = _proj_sb(xs, g, w_in)
      og = _sb_dec(page_table, bias, _to_pairs(q, db, n_tok, sb_heads),
                   _pad_tokens(kb, db, n_tok, page_rows), _pad_tokens(vb, db, n_tok, page_rows),
                   _to_pairs(gate, db, n_tok, sb_heads), cache, idx, sb_heads, n_tok)
      xs = _outproj(_from_pairs(og, db, n_tok, sb_heads), w_out, xs)
      sb_s.append(kv.reshape(db, n_tok, 2, sb_heads, dh))
    else:
      w_in = w_in_dil[idx].astype(_BF16)
      w_out = w_out_dil[idx].astype(_BF16)
      gq = q_norm_dil[idx].reshape(1, dh)
      gk = k_norm_dil[idx].reshape(1, dh)
      qs, ks, vs, kvw, gate = _proj_dil(xp, g, gq, gk, w_in, dils, s)
      os_, lses = [], []
      for gi, (win, dl) in enumerate(_DIL_GROUPS):
        o, lse = _dil_attn(slopes, qs[gi], ks[gi], vs[gi], dl, gi, win // dl)
        os_.append(o)
        lses.append(lse)
      xp = _dil_out(os_, lses, gate, w_out, xp, s)
      kvw = kvw.reshape(b, s, 2 * ng * hg * dh)
      for gi, (win, _) in enumerate(_DIL_GROUPS):
        rows = kvw[:, s - min(win, s):, gi * 2 * hg * dh:(gi + 1) * 2 * hg * dh]
        win_p[gi].append(rows.reshape(b, min(win, s), 2, hg, dh))
      ones = (1,) * ng
      qs, ks, vs, kvw, gate = _proj_dil(xs, g, gq, gk, w_in, ones, db * n_tok)
      cat = lambda parts: jnp.concatenate([a.reshape(db * n_tok, hg * dh) for a in parts], axis=1)
      qp = jnp.stack([_to_pairs(a.reshape(db * n_tok, hg * dh), db, n_tok, hg) for a in qs], axis=1)
      kvw = kvw.reshape(db, n_tok, ng, 2 * hg, dh)
      new_rows = [kvw[:, :, gi].reshape(db, n_tok * 2 * hg, dh) for gi in range(ng)]
      og, wins_s = _dil_dec(slopes, qp, _pad_tokens(cat(ks), db, n_tok, _LANES),
                            _pad_tokens(cat(vs), db, n_tok, _LANES),
                            _to_pairs(gate, db, n_tok, hg), new_rows, states_r, wins_s, idx,
                            _DIL_GROUPS, n_tok)
      xs = _outproj(_from_pairs(og, db, n_tok, hg), w_out, xs)
  outs = [xp.reshape(b, s, d_model), xs.reshape(db, n_tok, d_model),
          jnp.stack(sb_p, axis=1), jnp.stack(sb_s, axis=1)]
  for gi in range(ng):
    outs += [jnp.stack(win_p[gi], axis=1), wins_s[gi].reshape(states[gi].shape)]
  return tuple(outs)
```

```python
import functools

import jax
import jax.numpy as jnp
from jax import lax
from jax.experimental import pallas as pl
from jax.experimental.pallas import tpu as pltpu

_F32 = jnp.float32
_BF16 = jnp.bfloat16

_HEAD_DIM = 128
_DIL_GROUPS = ((128, 1), (512, 4), (2048, 16))
_DIL_HEADS_PER_GROUP = 4
_ALIBI_MAX_EXP = 8.0
_EPS = 1e-6
_N_MIXERS = 2
_LOG2E = 1.4426950408889634

_LANES = 128
_TM = 512
_TM_SB = 1024
_TN = 512
_TQ = 1024
_TK = 256
_SB_HEADS_PER_STEP = 2
_DEC_PAGES_PER_STEP = 4
_DIL_BLOCKS_PER_STEP = 4
_VMEM_LIMIT = 56 * 1024 * 1024


def _dot(a, b):
  return jnp.dot(a, b, preferred_element_type=_F32)


def _dot_nt(a, b):
  return lax.dot_general(a, b, (((1,), (1,)), ((), ())), preferred_element_type=_F32)


def _silu(g):
  return g / (1.0 + jnp.exp(-g))


def _softplus2_parts(z2):
  sign = jnp.int32(-2 ** 31)
  neg_abs = lax.bitcast_convert_type(lax.bitcast_convert_type(z2, jnp.int32) | sign, _F32)
  sp2 = jnp.maximum(z2, 0.0) + jnp.log(1.0 + jnp.exp2(neg_abs)) * _LOG2E
  return sp2, z2 - sp2


def _split_bf16(x):
  hi = x.astype(_BF16)
  lo = (x - hi.astype(_F32)).astype(_BF16)
  return hi, lo


def _rmsnorm_rows(x, g, eps):
  ms = jnp.mean(x * x, axis=-1, keepdims=True)
  return x * lax.rsqrt(ms + eps) * g


def _proj_sb_kernel(x_ref, g_ref, w_ref, q_ref, kv_ref, kb_ref, vb_ref, gate_ref, h_ref, *, ns):
  j = pl.program_id(1)

  @pl.when(j == 0)
  def _():
    h_ref[...] = _rmsnorm_rows(x_ref[...], g_ref[...], _EPS).astype(_BF16)

  acc = _dot(h_ref[...], w_ref[...])

  @pl.when(j < ns)
  def _():
    q_ref[...] = (acc * (_HEAD_DIM ** -0.5 * _LOG2E)).astype(_BF16)

  @pl.when((j >= ns) & (j < 3 * ns))
  def _():
    kv_ref[...] = acc

  @pl.when((j >= ns) & (j < 2 * ns))
  def _():
    kb_ref[...] = acc.astype(_BF16)

  @pl.when((j >= 2 * ns) & (j < 3 * ns))
  def _():
    vb_ref[...] = acc.astype(_BF16)

  @pl.when(j >= 3 * ns)
  def _():
    gate_ref[...] = acc


def _proj_sb(x, g, w):
  m, d = x.shape
  width = w.shape[1] // 4
  tm, tn = min(_TM_SB, m), min(_TN, width)
  ns = width // tn
  clip = lambda j, lo, n: jnp.clip(j - lo, 0, n - 1)
  return pl.pallas_call(
      functools.partial(_proj_sb_kernel, ns=ns),
      grid=(m // tm, 4 * ns),
      in_specs=[
          pl.BlockSpec((tm, d), lambda i, j: (i, 0)),
          pl.BlockSpec((1, d), lambda i, j: (0, 0)),
          pl.BlockSpec((d, tn), lambda i, j: (0, j)),
      ],
      out_specs=[
          pl.BlockSpec((tm, tn), lambda i, j: (i, clip(j, 0, ns))),
          pl.BlockSpec((tm, tn), lambda i, j: (i, clip(j, ns, 2 * ns))),
          pl.BlockSpec((tm, tn), lambda i, j: (i, clip(j, ns, ns))),
          pl.BlockSpec((tm, tn), lambda i, j: (i, clip(j, 2 * ns, ns))),
          pl.BlockSpec((tm, tn), lambda i, j: (i, clip(j, 3 * ns, ns))),
      ],
      out_shape=[
          jax.ShapeDtypeStruct((m, width), _BF16),
          jax.ShapeDtypeStruct((m, 2 * width), _F32),
          jax.ShapeDtypeStruct((m, width), _BF16),
          jax.ShapeDtypeStruct((m, width), _BF16),
          jax.ShapeDtypeStruct((m, width), _F32),
      ],
      scratch_shapes=[pltpu.VMEM((tm, d), _BF16)],
      compiler_params=pltpu.CompilerParams(
          dimension_semantics=("parallel", "arbitrary"), vmem_limit_bytes=_VMEM_LIMIT),
      name="proj_sb",
  )(x, g, w)


def _sb_attn_kernel(bias_ref, q_ref, k_ref, v_ref, gate_ref, o_ref, *, tq, tk, hp):
  hb = pl.program_id(1)
  qi = pl.program_id(2)
  dh = _HEAD_DIM
  nd = tq // tk
  row = lax.broadcasted_iota(jnp.int32, (tk, tk), 0)
  col = lax.broadcasted_iota(jnp.int32, (tk, tk), 1)
  cum = jnp.where(row > col, 1.0, 0.0).astype(_BF16)
  heads = [slice(c * dh, (c + 1) * dh) for c in range(hp)]
  bias2 = [bias_ref[hb * hp + c] * _LOG2E for c in range(hp)]

  def block(j, state, u):
    r0 = 0 if u is None else u * tk
    start = pl.multiple_of(j * tk, tk)
    out = []
    for c, cols in enumerate(heads):
      carry, acc = state[c]
      kb = k_ref[0, pl.ds(start, tk), cols]
      vb = v_ref[0, pl.ds(start, tk), cols]
      z2 = _dot_nt(q_ref[0, r0:, cols], kb) + bias2[c]
      sp2, ls2 = _softplus2_parts(z2)
      if u is not None:
        qrow = lax.broadcasted_iota(jnp.int32, (tq - r0, tk), 0)
        kcol = lax.broadcasted_iota(jnp.int32, (tq - r0, tk), 1)
        strict = kcol < qrow
        sp2 = jnp.where(strict, sp2, 0.0)
      res = _dot(sp2.astype(_BF16), cum)
      a = jnp.exp2(ls2 - res)
      if u is not None:
        a = jnp.where(strict, a, 0.0)
      pv = _dot(a.astype(_BF16), vb) * jnp.exp2(-carry[r0:])
      new_carry = carry[r0:] + res[:, :1] + sp2[:, :1]
      if r0:
        pv = jnp.concatenate([jnp.zeros((r0, dh), _F32), pv], axis=0)
        new_carry = jnp.concatenate([carry[:r0], new_carry], axis=0)
      out.append((new_carry, acc + pv))
    return tuple(out)

  state = tuple((jnp.zeros((tq, 1), _F32), jnp.zeros((tq, dh), _F32)) for _ in heads)
  for u in reversed(range(nd)):
    state = block(qi * nd + u, state, u)
  state = lax.fori_loop(0, qi * nd, lambda it, st: block(qi * nd - 1 - it, st, None), state)
  for c, cols in enumerate(heads):
    o_ref[0, :, cols] = (state[c][1] * _silu(gate_ref[0, :, cols])).astype(_BF16)


def _sb_attn(bias, q, k, v, gate, n_heads):
  b, s, width = q.shape
  tq = min(_TQ, s)
  tk = min(_TK, tq)
  hp = min(_SB_HEADS_PER_STEP, n_heads)
  w = hp * _HEAD_DIM
  return pl.pallas_call(
      functools.partial(_sb_attn_kernel, tq=tq, tk=tk, hp=hp),
      grid=(b, n_heads // hp, s // tq),
      in_specs=[
          pl.BlockSpec(memory_space=pltpu.SMEM),
          pl.BlockSpec((1, tq, w), lambda bi, h, qi: (bi, qi, h)),
          pl.BlockSpec((1, s, w), lambda bi, h, qi: (bi, 0, h)),
          pl.BlockSpec((1, s, w), lambda bi, h, qi: (bi, 0, h)),
          pl.BlockSpec((1, tq, w), lambda bi, h, qi: (bi, qi, h)),
      ],
      out_specs=pl.BlockSpec((1, tq, w), lambda bi, h, qi: (bi, qi, h)),
      out_shape=jax.ShapeDtypeStruct((b, s, width), _BF16),
      compiler_params=pltpu.CompilerParams(
          dimension_semantics=("parallel", "parallel", "arbitrary"),
          vmem_limit_bytes=_VMEM_LIMIT),
      name="sb_attn",
  )(bias, q, k, v, gate)


def _outproj_kernel(a_ref, w_ref, x_ref, o_ref):
  o_ref[...] = x_ref[...] + _dot(a_ref[...], w_ref[...])


def _outproj(a, w, x):
  m, k = a.shape
  n = w.shape[1]
  tm = min(_TM, m)
  return pl.pallas_call(
      _outproj_kernel,
      grid=(m // tm,),
      in_specs=[
          pl.BlockSpec((tm, k), lambda i: (i, 0)),
          pl.BlockSpec((k, n), lambda i: (0, 0)),
          pl.BlockSpec((tm, n), lambda i: (i, 0)),
      ],
      out_specs=pl.BlockSpec((tm, n), lambda i: (i, 0)),
      out_shape=jax.ShapeDtypeStruct((m, n), _F32),
      compiler_params=pltpu.CompilerParams(
          dimension_semantics=("parallel",), vmem_limit_bytes=_VMEM_LIMIT),
      name="outproj",
  )(a, w, x)


def _sb_dec_kernel(pt_ref, bias_ref, qp_ref, kn_ref, vn_ref, gate_ref, cache_ref, o_ref,
                   acc_ref, c_ref, kv_buf, sem, *, n_heads, n_tok, pages_per_step, page_rows,
                   layer_idx):
  seq = pl.program_id(0)
  step = pl.program_id(1)
  n_steps = pl.num_programs(1)
  n_pages = n_steps * pages_per_step
  n_pairs = n_heads // 2
  pr = 2 * n_tok
  rows = n_pairs * pr
  dh = _HEAD_DIM
  n_kv = 2 * n_heads

  def page_copies(w, slot):
    b_w = w // n_steps
    s_w = w % n_steps
    out = []
    for u in range(pages_per_step):
      page = pt_ref[b_w, n_pages - (s_w + 1) * pages_per_step + u]
      for r in range(n_kv):
        out.append(pltpu.make_async_copy(
            cache_ref.at[page, layer_idx, :, r, :],
            kv_buf.at[slot, r, pl.ds(u * page_rows, page_rows), :], sem.at[slot]))
    return out

  w = seq * n_steps + step
  slot = w % 2

  @pl.when(w == 0)
  def _():
    for i, cp in enumerate(page_copies(w, slot)):
      cp.start(priority=i % 2)

  @pl.when(w + 1 < pl.num_programs(0) * n_steps)
  def _():
    for i, cp in enumerate(page_copies(w + 1, 1 - slot)):
      cp.start(priority=i % 2)

  r2 = lax.broadcasted_iota(jnp.int32, (page_rows, 2 * page_rows), 0)
  c2 = lax.broadcasted_iota(jnp.int32, (page_rows, 2 * page_rows), 1)
  cum_ones = jnp.where((c2 >= page_rows) | (r2 > c2), 1.0, 0.0).astype(_BF16)
  top_o = lax.broadcasted_iota(jnp.int32, (pr, dh), 0) < n_tok

  def process(k_of, v_of, nb, mask, carry, accs):
    nk = nb * page_rows
    top = lax.broadcasted_iota(jnp.int32, (pr, nk), 0) < n_tok
    bias_col = jnp.concatenate(
        [jnp.full((n_tok, nk), bias_ref[hh] * _LOG2E, _F32) for hh in range(n_heads)], axis=0)
    zs = []
    for p in range(n_pairs):
      qq = qp_ref[0, p]
      zs.append(jnp.where(top, _dot_nt(qq, k_of(2 * p)), _dot_nt(qq, k_of(2 * p + 1))))
    z2 = jnp.concatenate(zs, axis=0) + bias_col
    sp2, ls2 = _softplus2_parts(z2)
    if mask is not None:
      sp2 = jnp.where(mask, sp2, 0.0)
    hi, lo = _split_bf16(sp2)
    sub = lambda x: [x[:, b * page_rows:(b + 1) * page_rows] for b in range(nb)]
    res = _dot(jnp.concatenate(sub(hi) + sub(lo), axis=0), cum_ones)
    res = res[:nb * rows] + res[nb * rows:]
    tots = [None] * nb
    for b in reversed(range(nb)):
      blk = res[b * rows:(b + 1) * rows]
      tots[b] = blk[:, :page_rows] + carry
      carry = carry + blk[:, page_rows:]
    a = jnp.exp2(ls2 - jnp.concatenate(tots, axis=1))
    if mask is not None:
      a = jnp.where(mask, a, 0.0)
    ab = a.astype(_BF16)
    new_accs = []
    for p in range(n_pairs):
      ap = ab[p * pr:(p + 1) * pr]
      new_accs.append(
          accs[p] + jnp.where(top_o, _dot(ap, v_of(2 * p)), _dot(ap, v_of(2 * p + 1))))
    return carry, new_accs

  @pl.when(step == 0)
  def _():
    tok = lax.broadcasted_iota(jnp.int32, (rows, page_rows), 0) % n_tok
    key = lax.broadcasted_iota(jnp.int32, (rows, page_rows), 1)
    carry, accs = process(
        lambda hh: kn_ref[0, :, hh * dh:(hh + 1) * dh],
        lambda hh: vn_ref[0, :, hh * dh:(hh + 1) * dh],
        1, key < tok, jnp.zeros((rows, page_rows), _F32),
        [jnp.zeros((pr, dh), _F32) for _ in range(n_pairs)])
    c_ref[...] = carry
    for p in range(n_pairs):
      acc_ref[p] = accs[p]

  for cp in page_copies(w, slot):
    cp.wait()

  carry, accs = process(
      lambda hh: kv_buf[slot, hh].astype(_BF16),
      lambda hh: kv_buf[slot, n_heads + hh].astype(_BF16),
      pages_per_step, None, c_ref[...], [acc_ref[p] for p in range(n_pairs)])
  c_ref[...] = carry
  for p in range(n_pairs):
    acc_ref[p] = accs[p]

  @pl.when(step == pl.num_programs(1) - 1)
  def _():
    for p in range(n_pairs):
      o_ref[0, p] = (accs[p] * _silu(gate_ref[0, p])).astype(_BF16)


def _sb_dec(page_table, bias, qp, kn, vn, gate_p, cache, layer_idx, n_heads, n_tok):
  db, n_pages = page_table.shape
  page_rows = cache.shape[2]
  pps = _DEC_PAGES_PER_STEP
  n_pairs = n_heads // 2
  pr = 2 * n_tok
  dh = _HEAD_DIM
  width = n_heads * dh
  per_seq = lambda shape: pl.BlockSpec((1,) + shape, lambda bi, s, pt: (bi,) + (0,) * len(shape))
  return pl.pallas_call(
      functools.partial(_sb_dec_kernel, n_heads=n_heads, n_tok=n_tok, pages_per_step=pps,
                        page_rows=page_rows, layer_idx=layer_idx),
      grid_spec=pltpu.PrefetchScalarGridSpec(
          num_scalar_prefetch=1,
          grid=(db, n_pages // pps),
          in_specs=[
              pl.BlockSpec(memory_space=pltpu.SMEM),
              per_seq((n_pairs, pr, dh)),
              per_seq((page_rows, width)),
              per_seq((page_rows, width)),
              per_seq((n_pairs, pr, dh)),
              pl.BlockSpec(memory_space=pl.ANY),
          ],
          out_specs=per_seq((n_pairs, pr, dh)),
          scratch_shapes=[
              pltpu.VMEM((n_pairs, pr, dh), _F32),
              pltpu.VMEM((n_pairs * pr, page_rows), _F32),
              pltpu.VMEM((2, 2 * n_heads, pps * page_rows, dh), _F32),
              pltpu.SemaphoreType.DMA((2,)),
          ],
      ),
      out_shape=jax.ShapeDtypeStruct((db, n_pairs, pr, dh), _BF16),
      compiler_params=pltpu.CompilerParams(
          dimension_semantics=("arbitrary", "arbitrary"), vmem_limit_bytes=_VMEM_LIMIT),
      name="sb_dec",
  )(page_table, bias, qp, kn, vn, gate_p, cache)


def _proj_dil_kernel(x_ref, g_ref, gq_ref, gk_ref, w_ref, *rest, dils, hg):
  ng = len(dils)
  q_refs = rest[:ng]
  k_refs = rest[ng:2 * ng]
  v_refs = rest[2 * ng:3 * ng]
  kvw_ref, gate_ref, h_ref, slab_ref = rest[3 * ng:]
  j = pl.program_id(1)
  dh = _HEAD_DIM
  tm = x_ref.shape[0]

  @pl.when(j == 0)
  def _():
    h_ref[...] = _rmsnorm_rows(x_ref[...], g_ref[...], _EPS).astype(_BF16)

  acc = _dot(h_ref[...], w_ref[...])

  def emit(dst_ref, d, heads):
    for c, xh in enumerate(heads):
      cols = slice(c * dh, (c + 1) * dh)
      if d == 1:
        dst_ref[0, 0, :, cols] = xh.astype(_BF16)
      else:
        slab_ref[c] = xh
        for r in range(d):
          dst_ref[0, r, :, cols] = slab_ref[c, pl.ds(r, tm // d, stride=d), :].astype(_BF16)

  def heads_of(norm_ref):
    out = []
    for c in range(hg):
      xh = acc[:, c * dh:(c + 1) * dh]
      if norm_ref is not None:
        xh = _rmsnorm_rows(xh, norm_ref[...], _EPS)
      out.append(xh)
    return out

  for g in range(ng):
    @pl.when(j == g)
    def _(g=g):
      emit(q_refs[g], dils[g], heads_of(gq_ref))

    @pl.when(j == ng + g)
    def _(g=g):
      heads = heads_of(gk_ref)
      for c, xh in enumerate(heads):
        kvw_ref[:, (2 * g * hg + c) * dh:(2 * g * hg + c + 1) * dh] = xh
      emit(k_refs[g], dils[g], heads)

    @pl.when(j == 2 * ng + g)
    def _(g=g):
      heads = heads_of(None)
      for c, xh in enumerate(heads):
        kvw_ref[:, ((2 * g + 1) * hg + c) * dh:((2 * g + 1) * hg + c + 1) * dh] = xh
      emit(v_refs[g], dils[g], heads)

  @pl.when(j == 3 * ng)
  def _():
    gate_ref[...] = acc


def _proj_dil(x, g, gq, gk, w, dils, rows_per_batch):
  m, d = x.shape
  ng = len(dils)
  hg = _DIL_HEADS_PER_GROUP
  gw = hg * _HEAD_DIM
  tm = min(_TM, rows_per_batch)
  nb = m // rows_per_batch
  tiles_per_batch = rows_per_batch // tm
  assert w.shape[1] == (3 * ng + 1) * gw and _TN == gw

  def cls_spec(dil):
    return pl.BlockSpec(
        (1, dil, tm // dil, gw),
        lambda i, j: (i // tiles_per_batch, 0, i % tiles_per_batch, 0))

  cls_shape = lambda dil: jax.ShapeDtypeStruct((nb, dil, rows_per_batch // dil, gw), _BF16)
  outs = pl.pallas_call(
      functools.partial(_proj_dil_kernel, dils=dils, hg=hg),
      grid=(m // tm, 3 * ng + 1),
      in_specs=[
          pl.BlockSpec((tm, d), lambda i, j: (i, 0)),
          pl.BlockSpec((1, d), lambda i, j: (0, 0)),
          pl.BlockSpec((1, _HEAD_DIM), lambda i, j: (0, 0)),
          pl.BlockSpec((1, _HEAD_DIM), lambda i, j: (0, 0)),
          pl.BlockSpec((d, gw), lambda i, j: (0, j)),
      ],
      out_specs=[cls_spec(dl) for dl in dils] * 3 + [
          pl.BlockSpec((tm, 2 * ng * gw), lambda i, j: (i, 0)),
          pl.BlockSpec((tm, gw), lambda i, j: (i, 0)),
      ],
      out_shape=[cls_shape(dl) for dl in dils] * 3 + [
          jax.ShapeDtypeStruct((m, 2 * ng * gw), _F32),
          jax.ShapeDtypeStruct((m, gw), _F32),
      ],
      scratch_shapes=[pltpu.VMEM((tm, d), _BF16), pltpu.VMEM((hg, tm, _HEAD_DIM), _F32)],
      compiler_params=pltpu.CompilerParams(
          dimension_semantics=("parallel", "arbitrary"), vmem_limit_bytes=_VMEM_LIMIT),
      name="proj_dil",
  )(x, g, gq, gk, w)
  return outs[:ng], outs[ng:2 * ng], outs[2 * ng:3 * ng], outs[3 * ng], outs[3 * ng + 1]


def _dil_attn_kernel(slope_ref, q_ref, kp_ref, kc_ref, vp_ref, vc_ref, o_ref, lse_ref,
                     *, dil, group, hg, blk, mb, scale):
  mstep = pl.program_id(1)
  r = pl.program_id(2)
  dh = _HEAD_DIM
  row = lax.broadcasted_iota(jnp.int32, (blk, 2 * blk), 0)
  col = lax.broadcasted_iota(jnp.int32, (blk, 2 * blk), 1)
  delta = blk + row - col
  in_band = (delta >= 0) & (delta <= blk)
  dist = (delta * dil).astype(_F32)
  for u in range(mb):
    valid = in_band & ((col >= blk) | (mstep > 0)) if u == 0 else in_band
    rows_u = slice(u * blk, (u + 1) * blk)
    for c in range(hg):
      cols = slice(c * dh, (c + 1) * dh)
      if u == 0:
        kprev, vprev = kp_ref[0, 0, :, cols], vp_ref[0, 0, :, cols]
      else:
        before = slice((u - 1) * blk, u * blk)
        kprev, vprev = kc_ref[0, 0, before, cols], vc_ref[0, 0, before, cols]
      kk = jnp.concatenate([kprev, kc_ref[0, 0, rows_u, cols]], axis=0)
      vv = jnp.concatenate([vprev, vc_ref[0, 0, rows_u, cols]], axis=0)
      s = _dot_nt(q_ref[0, 0, rows_u, cols], kk) * scale - slope_ref[group * hg + c] * dist
      s = jnp.where(valid, s, -jnp.inf)
      mx = jnp.max(s, axis=1, keepdims=True)
      p = jnp.exp(s - mx)
      den = jnp.sum(p, axis=1, keepdims=True)
      o = _dot((p / den).astype(_BF16), vv)
      lse = jnp.broadcast_to(mx + jnp.log(den), (blk, dh))
      if dil == 1:
        o_ref[0, c, rows_u, :] = o
        lse_ref[0, c, rows_u, :] = lse
      else:
        dst = pl.ds(u * blk * dil + r, blk, stride=dil)
        o_ref[0, c, dst, :] = o
        lse_ref[0, c, dst, :] = lse


def _dil_attn(slopes, q, k, v, dil, group, blk):
  b, _, sd, gw = q.shape
  hg = _DIL_HEADS_PER_GROUP
  dh = _HEAD_DIM
  s = sd * dil
  mb = min(_DIL_BLOCKS_PER_STEP, sd // blk)
  cur = pl.BlockSpec((1, 1, mb * blk, gw), lambda bi, m, r: (bi, r, m, 0))
  prev = pl.BlockSpec((1, 1, blk, gw), lambda bi, m, r: (bi, r, jnp.maximum(m * mb - 1, 0), 0))
  out = pl.BlockSpec((1, hg, mb * blk * dil, dh), lambda bi, m, r: (bi, 0, m, 0))
  return pl.pallas_call(
      functools.partial(_dil_attn_kernel, dil=dil, group=group, hg=hg, blk=blk, mb=mb,
                        scale=dh ** -0.5),
      grid=(b, sd // (mb * blk), dil),
      in_specs=[pl.BlockSpec(memory_space=pltpu.SMEM), cur, prev, cur, prev, cur],
      out_specs=[out, out],
      out_shape=[jax.ShapeDtypeStruct((b, hg, s, dh), _F32)] * 2,
      compiler_params=pltpu.CompilerParams(
          dimension_semantics=("parallel", "arbitrary", "arbitrary"),
          vmem_limit_bytes=_VMEM_LIMIT),
      name=f"dil_attn_g{group}",
  )(slopes, q, k, k, v, v)


def _dil_out_kernel(*refs, ng, hg):
  o_refs = refs[0:2 * ng:2]
  l_refs = refs[1:2 * ng:2]
  gate_ref, w_ref, x_ref, y_ref = refs[2 * ng:]
  dh = _HEAD_DIM
  parts = []
  for c in range(hg):
    lses = [l_refs[g][0, c] for g in range(ng)]
    mx = functools.reduce(jnp.maximum, lses)
    es = [jnp.exp(l - mx) for l in lses]
    den = functools.reduce(jnp.add, es)
    comb = functools.reduce(jnp.add, [(es[g] / den) * o_refs[g][0, c] for g in range(ng)])
    parts.append((comb * _silu(gate_ref[:, c * dh:(c + 1) * dh])).astype(_BF16))
  y_ref[...] = x_ref[...] + _dot(jnp.concatenate(parts, axis=1), w_ref[...])


def _dil_out(os_, lses, gate, w, x, rows_per_batch):
  m, n = x.shape
  ng = len(os_)
  hg = _DIL_HEADS_PER_GROUP
  dh = _HEAD_DIM
  tm = min(_TM, rows_per_batch)
  tpb = rows_per_batch // tm
  head_major = pl.BlockSpec((1, hg, tm, dh), lambda i: (i // tpb, 0, i % tpb, 0))
  args = []
  for o, l in zip(os_, lses):
    args += [o, l]
  return pl.pallas_call(
      functools.partial(_dil_out_kernel, ng=ng, hg=hg),
      grid=(m // tm,),
      in_specs=[head_major] * (2 * ng) + [
          pl.BlockSpec((tm, hg * dh), lambda i: (i, 0)),
          pl.BlockSpec((hg * dh, n), lambda i: (0, 0)),
          pl.BlockSpec((tm, n), lambda i: (i, 0)),
      ],
      out_specs=pl.BlockSpec((tm, n), lambda i: (i, 0)),
      out_shape=jax.ShapeDtypeStruct((m, n), _F32),
      compiler_params=pltpu.CompilerParams(
          dimension_semantics=("parallel",), vmem_limit_bytes=_VMEM_LIMIT),
      name="dil_out",
  )(*args, gate, w, x)


def _dil_dec_kernel(slope_ref, qp_ref, kn_ref, vn_ref, gate_ref, *rest, groups, hg, n_tok, pad_rows):
  ng = len(groups)
  new_refs = rest[:ng]
  st_refs = rest[ng:2 * ng]
  o_ref = rest[-1 - ng]
  win_refs = rest[-ng:]
  dh = _HEAD_DIM
  pr = 2 * n_tok
  n_pairs = hg // 2
  tok_col = lax.broadcasted_iota(jnp.int32, (pr, 1), 0) % n_tok
  top_col = lax.broadcasted_iota(jnp.int32, (pr, 1), 0) < n_tok
  top = lax.broadcasted_iota(jnp.int32, (pr, dh), 0) < n_tok
  row_stride = 2 * hg
  grp = 2 * n_tok
  half = n_tok * row_stride

  for g, (win, _) in enumerate(groups):
    st, dst = st_refs[g], win_refs[g]
    n_grp = win // grp
    cg = min(8, n_grp)

    def shift(m0, n_up, st=st, dst=dst):
      dst[0, 0, pl.ds(m0, cg), :half, :] = st[0, 0, pl.ds(m0, cg), half:, :]
      dst[0, 0, pl.ds(m0, n_up), half:, :] = st[0, 0, pl.ds(m0 + 1, n_up), :half, :]

    if n_grp > cg:
      def body(i, carry, shift=shift):
        shift(i * cg, cg)
        return carry
      lax.fori_loop(0, n_grp // cg - 1, body, 0)
    shift(n_grp - cg, cg - 1)
    dst[0, 0, n_grp - 1, half:, :] = new_refs[g][0]

  for p in range(n_pairs):
    outs, lses = [], []
    for g, (win, dil) in enumerate(groups):
      st = st_refs[g]
      qq = qp_ref[0, g, p]
      ha, hb = 2 * p, 2 * p + 1
      slope = jnp.where(top_col, slope_ref[g * hg + ha], slope_ref[g * hg + hb])
      n_cls = n_tok if dil % grp == 0 else grp
      n_keys = (win // grp) * n_cls

      def buf(off, st=st, n_cls=n_cls, n_keys=n_keys):
        rows = st[0, 0, :, pl.ds(off, n_cls, stride=row_stride), :]
        return rows.reshape(n_keys, dh).astype(_BF16)

      def new(ref, hh, g=g):
        return ref[0, :, (g * hg + hh) * dh:(g * hg + hh + 1) * dh]

      bcol = lax.broadcasted_iota(jnp.int32, (pr, n_keys), 1)
      brow = (bcol >> (n_cls.bit_length() - 1)) * grp + (bcol & (n_cls - 1))
      bdist = win + tok_col - brow
      bvalid = (brow >= tok_col) & ((bdist & (dil - 1)) == 0)
      sb_ = jnp.where(top_col, _dot_nt(qq, buf(ha)), _dot_nt(qq, buf(hb)))
      sb_ = sb_ * (dh ** -0.5) - slope * bdist.astype(_F32)
      sb_ = jnp.where(bvalid, sb_, -jnp.inf)
      ncol = lax.broadcasted_iota(jnp.int32, (pr, pad_rows), 1)
      ndist = tok_col - ncol
      nvalid = (ndist >= 0) & ((ndist & (dil - 1)) == 0)
      sn = jnp.where(top_col, _dot_nt(qq, new(kn_ref, ha)), _dot_nt(qq, new(kn_ref, hb)))
      sn = sn * (dh ** -0.5) - slope * ndist.astype(_F32)
      sn = jnp.where(nvalid, sn, -jnp.inf)

      mx = jnp.maximum(jnp.max(sb_, axis=1, keepdims=True), jnp.max(sn, axis=1, keepdims=True))
      pb = jnp.exp(sb_ - mx)
      pn = jnp.exp(sn - mx)
      den = jnp.sum(pb, axis=1, keepdims=True) + jnp.sum(pn, axis=1, keepdims=True)
      pb = (pb / den).astype(_BF16)
      pn = (pn / den).astype(_BF16)
      o = jnp.where(top, _dot(pb, buf(hg + ha)) + _dot(pn, new(vn_ref, ha)),
                    _dot(pb, buf(hg + hb)) + _dot(pn, new(vn_ref, hb)))
      outs.append(o)
      lses.append(mx + jnp.log(den))
    mx = functools.reduce(jnp.maximum, lses)
    es = [jnp.exp(l - mx) for l in lses]
    den = functools.reduce(jnp.add, es)
    comb = functools.reduce(jnp.add, [(es[g] / den) * outs[g] for g in range(ng)])
    o_ref[0, p] = (comb * _silu(gate_ref[0, p])).astype(_BF16)


def _dil_dec(slopes, qp, kn, vn, gate_p, new_rows, states, prev_wins, layer_idx, groups, n_tok):
  db = qp.shape[0]
  hg = _DIL_HEADS_PER_GROUP
  dh = _HEAD_DIM
  ng = len(groups)
  n_pairs = hg // 2
  pr = 2 * n_tok
  pad_rows = kn.shape[1]
  per_seq = lambda shape: pl.BlockSpec((1,) + shape, lambda bi: (bi,) + (0,) * len(shape))
  st_specs = [
      pl.BlockSpec((1, 1) + st.shape[2:], lambda bi: (bi, layer_idx, 0, 0, 0)) for st in states]
  in_specs = [
      pl.BlockSpec(memory_space=pltpu.SMEM),
      per_seq(qp.shape[1:]),
      per_seq(kn.shape[1:]),
      per_seq(vn.shape[1:]),
      per_seq((n_pairs, pr, dh)),
  ] + [per_seq(nr.shape[1:]) for nr in new_rows] + st_specs
  args = [slopes, qp, kn, vn, gate_p, *new_rows, *states]
  aliases = {}
  if prev_wins is not None:
    for g, pw in enumerate(prev_wins):
      aliases[len(args)] = 1 + g
      args.append(pw)
      in_specs.append(pl.BlockSpec(memory_space=pl.ANY))
  outs = pl.pallas_call(
      functools.partial(_dil_dec_kernel, groups=groups, hg=hg, n_tok=n_tok, pad_rows=pad_rows),
      grid=(db,),
      in_specs=in_specs,
      out_specs=[per_seq((n_pairs, pr, dh))] + st_specs,
      out_shape=[jax.ShapeDtypeStruct((db, n_pairs, pr, dh), _BF16)]
      + [jax.ShapeDtypeStruct(st.shape, _F32) for st in states],
      input_output_aliases=aliases,
      compiler_params=pltpu.CompilerParams(
          dimension_semantics=("arbitrary",), vmem_limit_bytes=_VMEM_LIMIT),
      name="dil_dec",
  )(*args)
  return outs[0], outs[1:]


def _to_pairs(a, db, n_tok, n_heads):
  a = a.reshape(db, n_tok, n_heads, _HEAD_DIM).transpose(0, 2, 1, 3)
  return a.reshape(db, n_heads // 2, 2 * n_tok, _HEAD_DIM)


def _from_pairs(a, db, n_tok, n_heads):
  a = a.reshape(db, n_heads, n_tok, _HEAD_DIM).transpose(0, 2, 1, 3)
  return a.reshape(db * n_tok, n_heads * _HEAD_DIM)


def _pad_tokens(a, db, n_tok, rows):
  a = a.reshape(db, n_tok, a.shape[-1])
  return jnp.pad(a, ((0, 0), (0, rows - n_tok), (0, 0)))


def kernel(x_prompt, x_sample, cache_sb, state_win_g0, state_win_g1, state_win_g2, page_table,
           norm_g, w_in_sb, sb_bias, w_out_sb, w_in_dil, q_norm_dil, k_norm_dil, w_out_dil):
  b, s, d_model = x_prompt.shape
  db, n_tok, _ = x_sample.shape
  depth = norm_g.shape[0]
  dh = _HEAD_DIM
  hg = _DIL_HEADS_PER_GROUP
  ng = len(_DIL_GROUPS)
  sb_heads = sb_bias.shape[1]
  page_rows = cache_sb.shape[2]
  states = (state_win_g0, state_win_g1, state_win_g2)
  dils = tuple(dl for _, dl in _DIL_GROUPS)
  for (win, dl), st in zip(_DIL_GROUPS, states):
    assert st.shape[2] == win and win // dl == _LANES and s % (dl * _LANES) == 0
  assert page_table.shape[1] * page_rows >= max(w for w, _ in _DIL_GROUPS)

  xp = x_prompt.reshape(b * s, d_model)
  xs = x_sample.reshape(db * n_tok, d_model)
  cache = cache_sb.reshape(cache_sb.shape[0], cache_sb.shape[1], page_rows, 2 * sb_heads, dh)
  grp = 2 * n_tok
  assert grp * 2 * hg == _LANES
  states_r = [st.reshape(db, st.shape[1], st.shape[2] // grp, grp * 2 * hg, dh) for st in states]
  heads = jnp.arange(1, ng * hg + 1, dtype=_F32)
  slopes = jnp.exp2(-_ALIBI_MAX_EXP * heads / (ng * hg))

  sb_p, sb_s = [], []
  win_p = [[] for _ in range(ng)]
  wins_s = None
  for layer in range(depth):
    idx = layer // _N_MIXERS
    g = norm_g[layer].reshape(1, d_model)
    if layer % _N_MIXERS == 0:
      w_in = w_in_sb[idx].astype(_BF16)
      w_out = w_out_sb[idx].astype(_BF16)
      bias = sb_bias[idx]
      q, kv, kb, vb, gate = _proj_sb(xp, g, w_in)
      r3 = lambda a: a.reshape(b, s, a.shape[-1])
      og = _sb_attn(bias, r3(q), r3(kb), r3(vb), r3(gate), sb_heads)
      xp = _outproj(og.reshape(b * s, -1), w_out, xp)
      sb_p.append(kv.reshape(b, s, 2, sb_heads, dh))
      q, kv, kb, vb, gate = _proj_sb(xs, g, w_in)
      og = _sb_dec(page_table, bias, _to_pairs(q, db, n_tok, sb_heads),
                   _pad_tokens(kb, db, n_tok, page_rows), _pad_tokens(vb, db, n_tok, page_rows),
                   _to_pairs(gate, db, n_tok, sb_heads), cache, idx, sb_heads, n_tok)
      xs = _outproj(_from_pairs(og, db, n_tok, sb_heads), w_out, xs)
      sb_s.append(kv.reshape(db, n_tok, 2, sb_heads, dh))
    else:
      w_in = w_in_dil[idx].astype(_BF16)
      w_out = w_out_dil[idx].astype(_BF16)
      gq = q_norm_dil[idx].reshape(1, dh)
      gk = k_norm_dil[idx].reshape(1, dh)
      qs, ks, vs, kvw, gate = _proj_dil(xp, g, gq, gk, w_in, dils, s)
      os_, lses = [], []
      for gi, (win, dl) in enumerate(_DIL_GROUPS):
        o, lse = _dil_attn(slopes, qs[gi], ks[gi], vs[gi], dl, gi, win // dl)
        os_.append(o)
        lses.append(lse)
      xp = _dil_out(os_, lses, gate, w_out, xp, s)
      kvw = kvw.reshape(b, s, 2 * ng * hg * dh)
      for gi, (win, _) in enumerate(_DIL_GROUPS):
        rows = kvw[:, s - min(win, s):, gi * 2 * hg * dh:(gi + 1) * 2 * hg * dh]
        win_p[gi].append(rows.reshape(b, min(win, s), 2, hg, dh))
      ones = (1,) * ng
      qs, ks, vs, kvw, gate = _proj_dil(xs, g, gq, gk, w_in, ones, db * n_tok)
      cat = lambda parts: jnp.concatenate([a.reshape(db * n_tok, hg * dh) for a in parts], axis=1)
      qp = jnp.stack([_to_pairs(a.reshape(db * n_tok, hg * dh), db, n_tok, hg) for a in qs], axis=1)
      kvw = kvw.reshape(db, n_tok, ng, 2 * hg, dh)
      new_rows = [kvw[:, :, gi].reshape(db, n_tok * 2 * hg, dh) for gi in range(ng)]
      og, wins_s = _dil_dec(slopes, qp, _pad_tokens(cat(ks), db, n_tok, _LANES),
                            _pad_tokens(cat(vs), db, n_tok, _LANES),
                            _to_pairs(gate, db, n_tok, hg), new_rows, states_r, wins_s, idx,
                            _DIL_GROUPS, n_tok)
      xs = _outproj(_from_pairs(og, db, n_tok, hg), w_out, xs)
  outs = [xp.reshape(b, s, d_model), xs.reshape(db, n_tok, d_model),
          jnp.stack(sb_p, axis=1), jnp.stack(sb_s, axis=1)]
  for gi in range(ng):
    outs += [jnp.stack(win_p[gi], axis=1), wins_s[gi].reshape(states[gi].shape)]
  return tuple(outs)
```

```python
import functools

import jax
import jax.numpy as jnp
from jax import lax
from jax.experimental import pallas as pl
from jax.experimental.pallas import tpu as pltpu

_F32 = jnp.float32
_BF16 = jnp.bfloat16

_HEAD_DIM = 128
_DIL_GROUPS = ((128, 1), (512, 4), (2048, 16))
_DIL_HEADS_PER_GROUP = 4
_ALIBI_MAX_EXP = 8.0
_EPS = 1e-6
_N_MIXERS = 2
_LOG2E = 1.4426950408889634

_LANES = 128
_TM = 512
_TM_SB = 1024
_TM_DIL = 1024
_TN = 512
_TQ = 1024
_TK = 256
_SB_HEADS_PER_STEP = 4
_DEC_PAGES_PER_STEP = 4
_DIL_BLOCKS_PER_STEP = 4
_VMEM_LIMIT = 56 * 1024 * 1024


def _dot(a, b):
  return jnp.dot(a, b, preferred_element_type=_F32)


def _dot_nt(a, b):
  return lax.dot_general(a, b, (((1,), (1,)), ((), ())), preferred_element_type=_F32)


def _silu(g):
  return g / (1.0 + jnp.exp(-g))


def _softplus2_parts(z2):
  sign = jnp.int32(-2 ** 31)
  neg_abs = lax.bitcast_convert_type(lax.bitcast_convert_type(z2, jnp.int32) | sign, _F32)
  sp2 = jnp.maximum(z2, 0.0) + jnp.log(1.0 + jnp.exp2(neg_abs)) * _LOG2E
  return sp2, z2 - sp2


def _split_bf16(x):
  hi = x.astype(_BF16)
  lo = (x - hi.astype(_F32)).astype(_BF16)
  return hi, lo


def _rmsnorm_rows(x, g, eps):
  ms = jnp.mean(x * x, axis=-1, keepdims=True)
  return x * lax.rsqrt(ms + eps) * g


def _proj_sb_kernel(x_ref, g_ref, w_ref, q_ref, kv_ref, kb_ref, vb_ref, gate_ref, h_ref, *, ns):
  j = pl.program_id(1)

  @pl.when(j == 0)
  def _():
    h_ref[...] = _rmsnorm_rows(x_ref[...], g_ref[...], _EPS).astype(_BF16)

  acc = _dot(h_ref[...], w_ref[...])

  @pl.when(j < ns)
  def _():
    q_ref[...] = (acc * (_HEAD_DIM ** -0.5 * _LOG2E)).astype(_BF16)

  @pl.when((j >= ns) & (j < 3 * ns))
  def _():
    kv_ref[...] = acc

  @pl.when((j >= ns) & (j < 2 * ns))
  def _():
    kb_ref[...] = acc.astype(_BF16)

  @pl.when((j >= 2 * ns) & (j < 3 * ns))
  def _():
    vb_ref[...] = acc.astype(_BF16)

  @pl.when(j >= 3 * ns)
  def _():
    gate_ref[...] = acc


def _proj_sb(x, g, w):
  m, d = x.shape
  width = w.shape[1] // 4
  tm, tn = min(_TM_SB, m), min(_TN, width)
  ns = width // tn
  clip = lambda j, lo, n: jnp.clip(j - lo, 0, n - 1)
  return pl.pallas_call(
      functools.partial(_proj_sb_kernel, ns=ns),
      grid=(m // tm, 4 * ns),
      in_specs=[
          pl.BlockSpec((tm, d), lambda i, j: (i, 0)),
          pl.BlockSpec((1, d), lambda i, j: (0, 0)),
          pl.BlockSpec((d, tn), lambda i, j: (0, j)),
      ],
      out_specs=[
          pl.BlockSpec((tm, tn), lambda i, j: (i, clip(j, 0, ns))),
          pl.BlockSpec((tm, tn), lambda i, j: (i, clip(j, ns, 2 * ns))),
          pl.BlockSpec((tm, tn), lambda i, j: (i, clip(j, ns, ns))),
          pl.BlockSpec((tm, tn), lambda i, j: (i, clip(j, 2 * ns, ns))),
          pl.BlockSpec((tm, tn), lambda i, j: (i, clip(j, 3 * ns, ns))),
      ],
      out_shape=[
          jax.ShapeDtypeStruct((m, width), _BF16),
          jax.ShapeDtypeStruct((m, 2 * width), _F32),
          jax.ShapeDtypeStruct((m, width), _BF16),
          jax.ShapeDtypeStruct((m, width), _BF16),
          jax.ShapeDtypeStruct((m, width), _F32),
      ],
      scratch_shapes=[pltpu.VMEM((tm, d), _BF16)],
      compiler_params=pltpu.CompilerParams(
          dimension_semantics=("parallel", "arbitrary"), vmem_limit_bytes=_VMEM_LIMIT),
      name="proj_sb",
  )(x, g, w)


def _sb_attn_kernel(bias_ref, q_ref, k_ref, v_ref, gate_ref, o_ref, *, tq, tk, hp):
  hb = pl.program_id(1)
  qi = pl.program_id(2)
  dh = _HEAD_DIM
  nd = tq // tk
  row = lax.broadcasted_iota(jnp.int32, (tk, tk), 0)
  col = lax.broadcasted_iota(jnp.int32, (tk, tk), 1)
  cum = jnp.where(row > col, 1.0, 0.0).astype(_BF16)
  heads = [slice(c * dh, (c + 1) * dh) for c in range(hp)]
  bias2 = [bias_ref[hb * hp + c] * _LOG2E for c in range(hp)]

  def block(j, state, u):
    r0 = 0 if u is None else u * tk
    start = pl.multiple_of(j * tk, tk)
    out = []
    for c, cols in enumerate(heads):
      carry, acc = state[c]
      kb = k_ref[0, pl.ds(start, tk), cols]
      vb = v_ref[0, pl.ds(start, tk), cols]
      z2 = _dot_nt(q_ref[0, r0:, cols], kb) + bias2[c]
      sp2, ls2 = _softplus2_parts(z2)
      if u is not None:
        qrow = lax.broadcasted_iota(jnp.int32, (tq - r0, tk), 0)
        kcol = lax.broadcasted_iota(jnp.int32, (tq - r0, tk), 1)
        strict = kcol < qrow
        sp2 = jnp.where(strict, sp2, 0.0)
      res = _dot(sp2.astype(_BF16), cum)
      a = jnp.exp2(ls2 - res)
      if u is not None:
        a = jnp.where(strict, a, 0.0)
      pv = _dot(a.astype(_BF16), vb) * jnp.exp2(-carry[r0:])
      new_carry = carry[r0:] + res[:, :1] + sp2[:, :1]
      if r0:
        pv = jnp.concatenate([jnp.zeros((r0, dh), _F32), pv], axis=0)
        new_carry = jnp.concatenate([carry[:r0], new_carry], axis=0)
      out.append((new_carry, acc + pv))
    return tuple(out)

  state = tuple((jnp.zeros((tq, 1), _F32), jnp.zeros((tq, dh), _F32)) for _ in heads)
  for u in reversed(range(nd)):
    state = block(qi * nd + u, state, u)
  state = lax.fori_loop(0, qi * nd, lambda it, st: block(qi * nd - 1 - it, st, None), state)
  for c, cols in enumerate(heads):
    o_ref[0, :, cols] = (state[c][1] * _silu(gate_ref[0, :, cols])).astype(_BF16)


def _sb_attn(bias, q, k, v, gate, n_heads):
  b, s, width = q.shape
  tq = min(_TQ, s)
  tk = min(_TK, tq)
  hp = min(_SB_HEADS_PER_STEP, n_heads)
  w = hp * _HEAD_DIM
  return pl.pallas_call(
      functools.partial(_sb_attn_kernel, tq=tq, tk=tk, hp=hp),
      grid=(b, n_heads // hp, s // tq),
      in_specs=[
          pl.BlockSpec(memory_space=pltpu.SMEM),
          pl.BlockSpec((1, tq, w), lambda bi, h, qi: (bi, qi, h)),
          pl.BlockSpec((1, s, w), lambda bi, h, qi: (bi, 0, h)),
          pl.BlockSpec((1, s, w), lambda bi, h, qi: (bi, 0, h)),
          pl.BlockSpec((1, tq, w), lambda bi, h, qi: (bi, qi, h)),
      ],
      out_specs=pl.BlockSpec((1, tq, w), lambda bi, h, qi: (bi, qi, h)),
      out_shape=jax.ShapeDtypeStruct((b, s, width), _BF16),
      compiler_params=pltpu.CompilerParams(
          dimension_semantics=("parallel", "parallel", "arbitrary"),
          vmem_limit_bytes=_VMEM_LIMIT),
      name="sb_attn",
  )(bias, q, k, v, gate)


def _outproj_kernel(a_ref, w_ref, x_ref, o_ref):
  o_ref[...] = x_ref[...] + _dot(a_ref[...], w_ref[...])


def _outproj(a, w, x):
  m, k = a.shape
  n = w.shape[1]
  tm = min(_TM, m)
  return pl.pallas_call(
      _outproj_kernel,
      grid=(m // tm,),
      in_specs=[
          pl.BlockSpec((tm, k), lambda i: (i, 0)),
          pl.BlockSpec((k, n), lambda i: (0, 0)),
          pl.BlockSpec((tm, n), lambda i: (i, 0)),
      ],
      out_specs=pl.BlockSpec((tm, n), lambda i: (i, 0)),
      out_shape=jax.ShapeDtypeStruct((m, n), _F32),
      compiler_params=pltpu.CompilerParams(
          dimension_semantics=("parallel",), vmem_limit_bytes=_VMEM_LIMIT),
      name="outproj",
  )(a, w, x)


def _sb_dec_kernel(pt_ref, bias_ref, qp_ref, kn_ref, vn_ref, gate_ref, cache_ref, o_ref,
                   acc_ref, c_ref, kv_buf, sem, *, n_heads, n_tok, pages_per_step, page_rows,
                   layer_idx):
  seq = pl.program_id(0)
  step = pl.program_id(1)
  n_steps = pl.num_programs(1)
  n_pages = n_steps * pages_per_step
  n_pairs = n_heads // 2
  pr = 2 * n_tok
  rows = n_pairs * pr
  dh = _HEAD_DIM
  n_kv = 2 * n_heads

  def page_copies(w, slot):
    b_w = w // n_steps
    s_w = w % n_steps
    out = []
    for u in range(pages_per_step):
      page = pt_ref[b_w, n_pages - (s_w + 1) * pages_per_step + u]
      for r in range(n_kv):
        out.append(pltpu.make_async_copy(
            cache_ref.at[page, layer_idx, :, r, :],
            kv_buf.at[slot, r, pl.ds(u * page_rows, page_rows), :], sem.at[slot]))
    return out

  w = seq * n_steps + step
  slot = w % 2

  @pl.when(w == 0)
  def _():
    for i, cp in enumerate(page_copies(w, slot)):
      cp.start(priority=i % 2)

  @pl.when(w + 1 < pl.num_programs(0) * n_steps)
  def _():
    for i, cp in enumerate(page_copies(w + 1, 1 - slot)):
      cp.start(priority=i % 2)

  r2 = lax.broadcasted_iota(jnp.int32, (page_rows, 2 * page_rows), 0)
  c2 = lax.broadcasted_iota(jnp.int32, (page_rows, 2 * page_rows), 1)
  cum_ones = jnp.where((c2 >= page_rows) | (r2 > c2), 1.0, 0.0).astype(_BF16)
  top_o = lax.broadcasted_iota(jnp.int32, (pr, dh), 0) < n_tok

  def process(k_of, v_of, nb, mask, carry, accs):
    nk = nb * page_rows
    top = lax.broadcasted_iota(jnp.int32, (pr, nk), 0) < n_tok
    bias_col = jnp.concatenate(
        [jnp.full((n_tok, nk), bias_ref[hh] * _LOG2E, _F32) for hh in range(n_heads)], axis=0)
    zs = []
    for p in range(n_pairs):
      qq = qp_ref[0, p]
      zs.append(jnp.where(top, _dot_nt(qq, k_of(2 * p)), _dot_nt(qq, k_of(2 * p + 1))))
    z2 = jnp.concatenate(zs, axis=0) + bias_col
    sp2, ls2 = _softplus2_parts(z2)
    if mask is not None:
      sp2 = jnp.where(mask, sp2, 0.0)
    hi, lo = _split_bf16(sp2)
    sub = lambda x: [x[:, b * page_rows:(b + 1) * page_rows] for b in range(nb)]
    res = _dot(jnp.concatenate(sub(hi) + sub(lo), axis=0), cum_ones)
    res = res[:nb * rows] + res[nb * rows:]
    tots = [None] * nb
    for b in reversed(range(nb)):
      blk = res[b * rows:(b + 1) * rows]
      tots[b] = blk[:, :page_rows] + carry
      carry = carry + blk[:, page_rows:]
    a = jnp.exp2(ls2 - jnp.concatenate(tots, axis=1))
    if mask is not None:
      a = jnp.where(mask, a, 0.0)
    ab = a.astype(_BF16)
    new_accs = []
    for p in range(n_pairs):
      ap = ab[p * pr:(p + 1) * pr]
      new_accs.append(
          accs[p] + jnp.where(top_o, _dot(ap, v_of(2 * p)), _dot(ap, v_of(2 * p + 1))))
    return carry, new_accs

  @pl.when(step == 0)
  def _():
    tok = lax.broadcasted_iota(jnp.int32, (rows, page_rows), 0) % n_tok
    key = lax.broadcasted_iota(jnp.int32, (rows, page_rows), 1)
    carry, accs = process(
        lambda hh: kn_ref[0, :, hh * dh:(hh + 1) * dh],
        lambda hh: vn_ref[0, :, hh * dh:(hh + 1) * dh],
        1, key < tok, jnp.zeros((rows, page_rows), _F32),
        [jnp.zeros((pr, dh), _F32) for _ in range(n_pairs)])
    c_ref[...] = carry
    for p in range(n_pairs):
      acc_ref[p] = accs[p]

  for cp in page_copies(w, slot):
    cp.wait()

  carry, accs = process(
      lambda hh: kv_buf[slot, hh].astype(_BF16),
      lambda hh: kv_buf[slot, n_heads + hh].astype(_BF16),
      pages_per_step, None, c_ref[...], [acc_ref[p] for p in range(n_pairs)])
  c_ref[...] = carry
  for p in range(n_pairs):
    acc_ref[p] = accs[p]

  @pl.when(step == pl.num_programs(1) - 1)
  def _():
    for p in range(n_pairs):
      o_ref[0, p] = (accs[p] * _silu(gate_ref[0, p])).astype(_BF16)


def _sb_dec(page_table, bias, qp, kn, vn, gate_p, cache, layer_idx, n_heads, n_tok):
  db, n_pages = page_table.shape
  page_rows = cache.shape[2]
  pps = _DEC_PAGES_PER_STEP
  n_pairs = n_heads // 2
  pr = 2 * n_tok
  dh = _HEAD_DIM
  width = n_heads * dh
  per_seq = lambda shape: pl.BlockSpec((1,) + shape, lambda bi, s, pt: (bi,) + (0,) * len(shape))
  return pl.pallas_call(
      functools.partial(_sb_dec_kernel, n_heads=n_heads, n_tok=n_tok, pages_per_step=pps,
                        page_rows=page_rows, layer_idx=layer_idx),
      grid_spec=pltpu.PrefetchScalarGridSpec(
          num_scalar_prefetch=1,
          grid=(db, n_pages // pps),
          in_specs=[
              pl.BlockSpec(memory_space=pltpu.SMEM),
              per_seq((n_pairs, pr, dh)),
              per_seq((page_rows, width)),
              per_seq((page_rows, width)),
              per_seq((n_pairs, pr, dh)),
              pl.BlockSpec(memory_space=pl.ANY),
          ],
          out_specs=per_seq((n_pairs, pr, dh)),
          scratch_shapes=[
              pltpu.VMEM((n_pairs, pr, dh), _F32),
              pltpu.VMEM((n_pairs * pr, page_rows), _F32),
              pltpu.VMEM((2, 2 * n_heads, pps * page_rows, dh), _F32),
              pltpu.SemaphoreType.DMA((2,)),
          ],
      ),
      out_shape=jax.ShapeDtypeStruct((db, n_pairs, pr, dh), _BF16),
      compiler_params=pltpu.CompilerParams(
          dimension_semantics=("arbitrary", "arbitrary"), vmem_limit_bytes=_VMEM_LIMIT),
      name="sb_dec",
  )(page_table, bias, qp, kn, vn, gate_p, cache)


def _rmsnorm_kernel(x_ref, g_ref, h_ref):
  h_ref[...] = _rmsnorm_rows(x_ref[...], g_ref[...], _EPS).astype(_BF16)


def _rmsnorm(x, g):
  m, d = x.shape
  tm = min(_TM_DIL, m)
  return pl.pallas_call(
      _rmsnorm_kernel,
      grid=(m // tm,),
      in_specs=[pl.BlockSpec((tm, d), lambda i: (i, 0)), pl.BlockSpec((1, d), lambda i: (0, 0))],
      out_specs=pl.BlockSpec((tm, d), lambda i: (i, 0)),
      out_shape=jax.ShapeDtypeStruct((m, d), _BF16),
      compiler_params=pltpu.CompilerParams(
          dimension_semantics=("parallel",), vmem_limit_bytes=_VMEM_LIMIT),
      name="rmsnorm",
  )(x, g)


def _proj_dil_kernel(h_ref, gq_ref, gk_ref, w_ref, *rest, dils, hg):
  ng = len(dils)
  q_refs = rest[:ng]
  k_refs = rest[ng:2 * ng]
  v_refs = rest[2 * ng:3 * ng]
  kvw_ref, gate_ref, slab_ref = rest[3 * ng:]
  j = pl.program_id(0)
  dh = _HEAD_DIM
  tm = h_ref.shape[0]

  acc = _dot(h_ref[...], w_ref[...])

  def emit(dst_ref, d, heads):
    for c, xh in enumerate(heads):
      cols = slice(c * dh, (c + 1) * dh)
      if d == 1:
        dst_ref[0, 0, :, cols] = xh.astype(_BF16)
      else:
        slab_ref[c] = xh
        for r in range(d):
          dst_ref[0, r, :, cols] = slab_ref[c, pl.ds(r, tm // d, stride=d), :].astype(_BF16)

  def heads_of(norm_ref):
    out = []
    for c in range(hg):
      xh = acc[:, c * dh:(c + 1) * dh]
      if norm_ref is not None:
        xh = _rmsnorm_rows(xh, norm_ref[...], _EPS)
      out.append(xh)
    return out

  for g in range(ng):
    @pl.when(j == g)
    def _(g=g):
      emit(q_refs[g], dils[g], heads_of(gq_ref))

    @pl.when(j == ng + g)
    def _(g=g):
      heads = heads_of(gk_ref)
      for c, xh in enumerate(heads):
        kvw_ref[:, c * dh:(c + 1) * dh] = xh
      emit(k_refs[g], dils[g], heads)

    @pl.when(j == 2 * ng + g)
    def _(g=g):
      heads = heads_of(None)
      for c, xh in enumerate(heads):
        kvw_ref[:, c * dh:(c + 1) * dh] = xh
      emit(v_refs[g], dils[g], heads)

  @pl.when(j == 3 * ng)
  def _():
    gate_ref[...] = acc


def _proj_dil(x, g, gq, gk, w, dils, rows_per_batch):
  m, d = x.shape
  ng = len(dils)
  hg = _DIL_HEADS_PER_GROUP
  gw = hg * _HEAD_DIM
  tm = min(_TM_DIL, rows_per_batch)
  nb = m // rows_per_batch
  tiles_per_batch = rows_per_batch // tm
  ni = m // tm
  assert w.shape[1] == (3 * ng + 1) * gw and _TN == gw
  h = _rmsnorm(x, g)

  def tile_of(j, i, section):
    return jnp.where(j < section, 0, jnp.where(j == section, i, ni - 1))

  def cls_spec(dil, section):
    def index(j, i):
      t = tile_of(j, i, section)
      return (t // tiles_per_batch, 0, t % tiles_per_batch, 0)
    return pl.BlockSpec((1, dil, tm // dil, gw), index)

  def kvw_index(j, i):
    jj = jnp.clip(j, ng, 3 * ng - 1)
    t = jnp.where(j < ng, 0, jnp.where(j < 3 * ng, i, ni - 1))
    col = jnp.where(jj < 2 * ng, 2 * (jj - ng), 2 * (jj - 2 * ng) + 1)
    return (t, col)

  cls_shape = lambda dil: jax.ShapeDtypeStruct((nb, dil, rows_per_batch // dil, gw), _BF16)
  outs = pl.pallas_call(
      functools.partial(_proj_dil_kernel, dils=dils, hg=hg),
      grid=(3 * ng + 1, ni),
      in_specs=[
          pl.BlockSpec((tm, d), lambda j, i: (i, 0)),
          pl.BlockSpec((1, _HEAD_DIM), lambda j, i: (0, 0)),
          pl.BlockSpec((1, _HEAD_DIM), lambda j, i: (0, 0)),
          pl.BlockSpec((d, gw), lambda j, i: (0, j)),
      ],
      out_specs=[cls_spec(dl, sec * ng + gi) for sec in range(3) for gi, dl in enumerate(dils)] + [
          pl.BlockSpec((tm, gw), kvw_index),
          pl.BlockSpec((tm, gw), lambda j, i: (tile_of(j, i, 3 * ng), 0)),
      ],
      out_shape=[cls_shape(dl) for dl in dils] * 3 + [
          jax.ShapeDtypeStruct((m, 2 * ng * gw), _F32),
          jax.ShapeDtypeStruct((m, gw), _F32),
      ],
      scratch_shapes=[pltpu.VMEM((hg, tm, _HEAD_DIM), _F32)],
      compiler_params=pltpu.CompilerParams(
          dimension_semantics=("arbitrary", "arbitrary"), vmem_limit_bytes=_VMEM_LIMIT),
      name="proj_dil",
  )(h, gq, gk, w)
  return outs[:ng], outs[ng:2 * ng], outs[2 * ng:3 * ng], outs[3 * ng], outs[3 * ng + 1]


def _dil_attn_kernel(slope_ref, q_ref, kp_ref, kc_ref, vp_ref, vc_ref, o_ref, lse_ref,
                     *, dil, group, hg, blk, mb, scale):
  mstep = pl.program_id(1)
  r = pl.program_id(2)
  dh = _HEAD_DIM
  row = lax.broadcasted_iota(jnp.int32, (blk, 2 * blk), 0)
  col = lax.broadcasted_iota(jnp.int32, (blk, 2 * blk), 1)
  delta = blk + row - col
  in_band = (delta >= 0) & (delta <= blk)
  dist = (delta * dil).astype(_F32)
  for u in range(mb):
    valid = in_band & ((col >= blk) | (mstep > 0)) if u == 0 else in_band
    rows_u = slice(u * blk, (u + 1) * blk)
    for c in range(hg):
      cols = slice(c * dh, (c + 1) * dh)
      if u == 0:
        kprev, vprev = kp_ref[0, 0, :, cols], vp_ref[0, 0, :, cols]
      else:
        before = slice((u - 1) * blk, u * blk)
        kprev, vprev = kc_ref[0, 0, before, cols], vc_ref[0, 0, before, cols]
      kk = jnp.concatenate([kprev, kc_ref[0, 0, rows_u, cols]], axis=0)
      vv = jnp.concatenate([vprev, vc_ref[0, 0, rows_u, cols]], axis=0)
      s = _dot_nt(q_ref[0, 0, rows_u, cols], kk) * scale - slope_ref[group * hg + c] * dist
      s = jnp.where(valid, s, -jnp.inf)
      mx = jnp.max(s, axis=1, keepdims=True)
      p = jnp.exp(s - mx)
      den = jnp.sum(p, axis=1, keepdims=True)
      o = _dot((p / den).astype(_BF16), vv)
      lse = jnp.broadcast_to(mx + jnp.log(den), (blk, dh))
      if dil == 1:
        o_ref[0, c, rows_u, :] = o
        lse_ref[0, c, rows_u, :] = lse
      else:
        dst = pl.ds(u * blk * dil + r, blk, stride=dil)
        o_ref[0, c, dst, :] = o
        lse_ref[0, c, dst, :] = lse


def _dil_attn(slopes, q, k, v, dil, group, blk):
  b, _, sd, gw = q.shape
  hg = _DIL_HEADS_PER_GROUP
  dh = _HEAD_DIM
  s = sd * dil
  mb = min(_DIL_BLOCKS_PER_STEP, sd // blk)
  cur = pl.BlockSpec((1, 1, mb * blk, gw), lambda bi, m, r: (bi, r, m, 0))
  prev = pl.BlockSpec((1, 1, blk, gw), lambda bi, m, r: (bi, r, jnp.maximum(m * mb - 1, 0), 0))
  out = pl.BlockSpec((1, hg, mb * blk * dil, dh), lambda bi, m, r: (bi, 0, m, 0))
  return pl.pallas_call(
      functools.partial(_dil_attn_kernel, dil=dil, group=group, hg=hg, blk=blk, mb=mb,
                        scale=dh ** -0.5),
      grid=(b, sd // (mb * blk), dil),
      in_specs=[pl.BlockSpec(memory_space=pltpu.SMEM), cur, prev, cur, prev, cur],
      out_specs=[out, out],
      out_shape=[jax.ShapeDtypeStruct((b, hg, s, dh), _F32)] * 2,
      compiler_params=pltpu.CompilerParams(
          dimension_semantics=("parallel", "arbitrary", "arbitrary"),
          vmem_limit_bytes=_VMEM_LIMIT),
      name=f"dil_attn_g{group}",
  )(slopes, q, k, k, v, v)


def _dil_out_kernel(*refs, ng, hg):
  o_refs = refs[0:2 * ng:2]
  l_refs = refs[1:2 * ng:2]
  gate_ref, w_ref, x_ref, y_ref = refs[2 * ng:]
  dh = _HEAD_DIM
  parts = []
  for c in range(hg):
    lses = [l_refs[g][0, c] for g in range(ng)]
    mx = functools.reduce(jnp.maximum, lses)
    es = [jnp.exp(l - mx) for l in lses]
    den = functools.reduce(jnp.add, es)
    comb = functools.reduce(jnp.add, [(es[g] / den) * o_refs[g][0, c] for g in range(ng)])
    parts.append((comb * _silu(gate_ref[:, c * dh:(c + 1) * dh])).astype(_BF16))
  y_ref[...] = x_ref[...] + _dot(jnp.concatenate(parts, axis=1), w_ref[...])


def _dil_out(os_, lses, gate, w, x, rows_per_batch):
  m, n = x.shape
  ng = len(os_)
  hg = _DIL_HEADS_PER_GROUP
  dh = _HEAD_DIM
  tm = min(_TM, rows_per_batch)
  tpb = rows_per_batch // tm
  head_major = pl.BlockSpec((1, hg, tm, dh), lambda i: (i // tpb, 0, i % tpb, 0))
  args = []
  for o, l in zip(os_, lses):
    args += [o, l]
  return pl.pallas_call(
      functools.partial(_dil_out_kernel, ng=ng, hg=hg),
      grid=(m // tm,),
      in_specs=[head_major] * (2 * ng) + [
          pl.BlockSpec((tm, hg * dh), lambda i: (i, 0)),
          pl.BlockSpec((hg * dh, n), lambda i: (0, 0)),
          pl.BlockSpec((tm, n), lambda i: (i, 0)),
      ],
      out_specs=pl.BlockSpec((tm, n), lambda i: (i, 0)),
      out_shape=jax.ShapeDtypeStruct((m, n), _F32),
      compiler_params=pltpu.CompilerParams(
          dimension_semantics=("parallel",), vmem_limit_bytes=_VMEM_LIMIT),
      name="dil_out",
  )(*args, gate, w, x)


def _dil_dec_kernel(slope_ref, qp_ref, kn_ref, vn_ref, gate_ref, *rest, groups, hg, n_tok, pad_rows):
  ng = len(groups)
  new_refs = rest[:ng]
  st_refs = rest[ng:2 * ng]
  o_ref = rest[-1 - ng]
  win_refs = rest[-ng:]
  dh = _HEAD_DIM
  pr = 2 * n_tok
  n_pairs = hg // 2
  tok_col = lax.broadcasted_iota(jnp.int32, (pr, 1), 0) % n_tok
  top_col = lax.broadcasted_iota(jnp.int32, (pr, 1), 0) < n_tok
  top = lax.broadcasted_iota(jnp.int32, (pr, dh), 0) < n_tok
  row_stride = 2 * hg
  grp = 2 * n_tok
  half = n_tok * row_stride

  for g, (win, _) in enumerate(groups):
    st, dst = st_refs[g], win_refs[g]
    n_grp = win // grp
    cg = min(8, n_grp)

    def shift(m0, n_up, st=st, dst=dst):
      dst[0, 0, pl.ds(m0, cg), :half, :] = st[0, 0, pl.ds(m0, cg), half:, :]
      dst[0, 0, pl.ds(m0, n_up), half:, :] = st[0, 0, pl.ds(m0 + 1, n_up), :half, :]

    if n_grp > cg:
      def body(i, carry, shift=shift):
        shift(i * cg, cg)
        return carry
      lax.fori_loop(0, n_grp // cg - 1, body, 0)
    shift(n_grp - cg, cg - 1)
    dst[0, 0, n_grp - 1, half:, :] = new_refs[g][0]

  for p in range(n_pairs):
    outs, lses = [], []
    for g, (win, dil) in enumerate(groups):
      st = st_refs[g]
      qq = qp_ref[0, g, p]
      ha, hb = 2 * p, 2 * p + 1
      slope = jnp.where(top_col, slope_ref[g * hg + ha], slope_ref[g * hg + hb])
      n_cls = n_tok if dil % grp == 0 else grp
      n_keys = (win // grp) * n_cls

      def buf(off, st=st, n_cls=n_cls, n_keys=n_keys):
        rows = st[0, 0, :, pl.ds(off, n_cls, stride=row_stride), :]
        return rows.reshape(n_keys, dh).astype(_BF16)

      def new(ref, hh, g=g):
        return ref[0, :, (g * hg + hh) * dh:(g * hg + hh + 1) * dh]

      bcol = lax.broadcasted_iota(jnp.int32, (pr, n_keys), 1)
      brow = (bcol >> (n_cls.bit_length() - 1)) * grp + (bcol & (n_cls - 1))
      bdist = win + tok_col - brow
      bvalid = (brow >= tok_col) & ((bdist & (dil - 1)) == 0)
      sb_ = jnp.where(top_col, _dot_nt(qq, buf(ha)), _dot_nt(qq, buf(hb)))
      sb_ = sb_ * (dh ** -0.5) - slope * bdist.astype(_F32)
      sb_ = jnp.where(bvalid, sb_, -jnp.inf)
      ncol = lax.broadcasted_iota(jnp.int32, (pr, pad_rows), 1)
      ndist = tok_col - ncol
      nvalid = (ndist >= 0) & ((ndist & (dil - 1)) == 0)
      sn = jnp.where(top_col, _dot_nt(qq, new(kn_ref, ha)), _dot_nt(qq, new(kn_ref, hb)))
      sn = sn * (dh ** -0.5) - slope * ndist.astype(_F32)
      sn = jnp.where(nvalid, sn, -jnp.inf)

      mx = jnp.maximum(jnp.max(sb_, axis=1, keepdims=True), jnp.max(sn, axis=1, keepdims=True))
      pb = jnp.exp(sb_ - mx)
      pn = jnp.exp(sn - mx)
      den = jnp.sum(pb, axis=1, keepdims=True) + jnp.sum(pn, axis=1, keepdims=True)
      pb = (pb / den).astype(_BF16)
      pn = (pn / den).astype(_BF16)
      o = jnp.where(top, _dot(pb, buf(hg + ha)) + _dot(pn, new(vn_ref, ha)),
                    _dot(pb, buf(hg + hb)) + _dot(pn, new(vn_ref, hb)))
      outs.append(o)
      lses.append(mx + jnp.log(den))
    mx = functools.reduce(jnp.maximum, lses)
    es = [jnp.exp(l - mx) for l in lses]
    den = functools.reduce(jnp.add, es)
    comb = functools.reduce(jnp.add, [(es[g] / den) * outs[g] for g in range(ng)])
    o_ref[0, p] = (comb * _silu(gate_ref[0, p])).astype(_BF16)


def _dil_dec(slopes, qp, kn, vn, gate_p, new_rows, states, prev_wins, layer_idx, groups, n_tok):
  db = qp.shape[0]
  hg = _DIL_HEADS_PER_GROUP
  dh = _HEAD_DIM
  ng = len(groups)
  n_pairs = hg // 2
  pr = 2 * n_tok
  pad_rows = kn.shape[1]
  per_seq = lambda shape: pl.BlockSpec((1,) + shape, lambda bi: (bi,) + (0,) * len(shape))
  st_specs = [
      pl.BlockSpec((1, 1) + st.shape[2:], lambda bi: (bi, layer_idx, 0, 0, 0)) for st in states]
  in_specs = [
      pl.BlockSpec(memory_space=pltpu.SMEM),
      per_seq(qp.shape[1:]),
      per_seq(kn.shape[1:]),
      per_seq(vn.shape[1:]),
      per_seq((n_pairs, pr, dh)),
  ] + [per_seq(nr.shape[1:]) for nr in new_rows] + st_specs
  args = [slopes, qp, kn, vn, gate_p, *new_rows, *states]
  aliases = {}
  if prev_wins is not None:
    for g, pw in enumerate(prev_wins):
      aliases[len(args)] = 1 + g
      args.append(pw)
      in_specs.append(pl.BlockSpec(memory_space=pl.ANY))
  outs = pl.pallas_call(
      functools.partial(_dil_dec_kernel, groups=groups, hg=hg, n_tok=n_tok, pad_rows=pad_rows),
      grid=(db,),
      in_specs=in_specs,
      out_specs=[per_seq((n_pairs, pr, dh))] + st_specs,
      out_shape=[jax.ShapeDtypeStruct((db, n_pairs, pr, dh), _BF16)]
      + [jax.ShapeDtypeStruct(st.shape, _F32) for st in states],
      input_output_aliases=aliases,
      compiler_params=pltpu.CompilerParams(
          dimension_semantics=("arbitrary",), vmem_limit_bytes=_VMEM_LIMIT),
      name="dil_dec",
  )(*args)
  return outs[0], outs[1:]


def _to_pairs(a, db, n_tok, n_heads):
  a = a.reshape(db, n_tok, n_heads, _HEAD_DIM).transpose(0, 2, 1, 3)
  return a.reshape(db, n_heads // 2, 2 * n_tok, _HEAD_DIM)


def _from_pairs(a, db, n_tok, n_heads):
  a = a.reshape(db, n_heads, n_tok, _HEAD_DIM).transpose(0, 2, 1, 3)
  return a.reshape(db * n_tok, n_heads * _HEAD_DIM)


def _pad_tokens(a, db, n_tok, rows):
  a = a.reshape(db, n_tok, a.shape[-1])
  return jnp.pad(a, ((0, 0), (0, rows - n_tok), (0, 0)))


def kernel(x_prompt, x_sample, cache_sb, state_win_g0, state_win_g1, state_win_g2, page_table,
           norm_g, w_in_sb, sb_bias, w_out_sb, w_in_dil, q_norm_dil, k_norm_dil, w_out_dil):
  b, s, d_model = x_prompt.shape
  db, n_tok, _ = x_sample.shape
  depth = norm_g.shape[0]
  dh = _HEAD_DIM
  hg = _DIL_HEADS_PER_GROUP
  ng = len(_DIL_GROUPS)
  sb_heads = sb_bias.shape[1]
  page_rows = cache_sb.shape[2]
  states = (state_win_g0, state_win_g1, state_win_g2)
  dils = tuple(dl for _, dl in _DIL_GROUPS)
  for (win, dl), st in zip(_DIL_GROUPS, states):
    assert st.shape[2] == win and win // dl == _LANES and s % (dl * _LANES) == 0
  assert page_table.shape[1] * page_rows >= max(w for w, _ in _DIL_GROUPS)

  xp = x_prompt.reshape(b * s, d_model)
  xs = x_sample.reshape(db * n_tok, d_model)
  cache = cache_sb.reshape(cache_sb.shape[0], cache_sb.shape[1], page_rows, 2 * sb_heads, dh)
  grp = 2 * n_tok
  assert grp * 2 * hg == _LANES
  states_r = [st.reshape(db, st.shape[1], st.shape[2] // grp, grp * 2 * hg, dh) for st in states]
  heads = jnp.arange(1, ng * hg + 1, dtype=_F32)
  slopes = jnp.exp2(-_ALIBI_MAX_EXP * heads / (ng * hg))

  sb_p, sb_s = [], []
  win_p = [[] for _ in range(ng)]
  wins_s = None
  for layer in range(depth):
    idx = layer // _N_MIXERS
    g = norm_g[layer].reshape(1, d_model)
    if layer % _N_MIXERS == 0:
      w_in = w_in_sb[idx].astype(_BF16)
      w_out = w_out_sb[idx].astype(_BF16)
      bias = sb_bias[idx]
      q, kv, kb, vb, gate = _proj_sb(xp, g, w_in)
      r3 = lambda a: a.reshape(b, s, a.shape[-1])
      og = _sb_attn(bias, r3(q), r3(kb), r3(vb), r3(gate), sb_heads)
      xp = _outproj(og.reshape(b * s, -1), w_out, xp)
      sb_p.append(kv.reshape(b, s, 2, sb_heads, dh))
      q, kv, kb, vb, gate = _proj_sb(xs, g, w_in)
      og = _sb_dec(page_table, bias, _to_pairs(q, db, n_tok, sb_heads),
                   _pad_tokens(kb, db, n_tok, page_rows), _pad_tokens(vb, db, n_tok, page_rows),
                   _to_pairs(gate, db, n_tok, sb_heads), cache, idx, sb_heads, n_tok)
      xs = _outproj(_from_pairs(og, db, n_tok, sb_heads), w_out, xs)
      sb_s.append(kv.reshape(db, n_tok, 2, sb_heads, dh))
    else:
      w_in = w_in_dil[idx].astype(_BF16)
      w_out = w_out_dil[idx].astype(_BF16)
      gq = q_norm_dil[idx].reshape(1, dh)
      gk = k_norm_dil[idx].reshape(1, dh)
      qs, ks, vs, kvw, gate = _proj_dil(xp, g, gq, gk, w_in, dils, s)
      os_, lses = [], []
      for gi, (win, dl) in enumerate(_DIL_GROUPS):
        o, lse = _dil_attn(slopes, qs[gi], ks[gi], vs[gi], dl, gi, win // dl)
        os_.append(o)
        lses.append(lse)
      xp = _dil_out(os_, lses, gate, w_out, xp, s)
      kvw = kvw.reshape(b, s, 2 * ng * hg * dh)
      for gi, (win, _) in enumerate(_DIL_GROUPS):
        rows = kvw[:, s - min(win, s):, gi * 2 * hg * dh:(gi + 1) * 2 * hg * dh]
        win_p[gi].append(rows.reshape(b, min(win, s), 2, hg, dh))
      ones = (1,) * ng
      qs, ks, vs, kvw, gate = _proj_dil(xs, g, gq, gk, w_in, ones, db * n_tok)
      cat = lambda parts: jnp.concatenate([a.reshape(db * n_tok, hg * dh) for a in parts], axis=1)
      qp = jnp.stack([_to_pairs(a.reshape(db * n_tok, hg * dh), db, n_tok, hg) for a in qs], axis=1)
      kvw = kvw.reshape(db, n_tok, ng, 2 * hg, dh)
      new_rows = [kvw[:, :, gi].reshape(db, n_tok * 2 * hg, dh) for gi in range(ng)]
      og, wins_s = _dil_dec(slopes, qp, _pad_tokens(cat(ks), db, n_tok, _LANES),
                            _pad_tokens(cat(vs), db, n_tok, _LANES),
                            _to_pairs(gate, db, n_tok, hg), new_rows, states_r, wins_s, idx,
                            _DIL_GROUPS, n_tok)
      xs = _outproj(_from_pairs(og, db, n_tok, hg), w_out, xs)
  outs = [xp.reshape(b, s, d_model), xs.reshape(db, n_tok, d_model),
          jnp.stack(sb_p, axis=1), jnp.stack(sb_s, axis=1)]
  for gi in range(ng):
    outs += [jnp.stack(win_p[gi], axis=1), wins_s[gi].reshape(states[gi].shape)]
  return tuple(outs)
```

```python
import functools

import jax
import jax.numpy as jnp
from jax import lax
from jax.experimental import pallas as pl
from jax.experimental.pallas import tpu as pltpu

_F32 = jnp.float32
_BF16 = jnp.bfloat16

_HEAD_DIM = 128
_DIL_GROUPS = ((128, 1), (512, 4), (2048, 16))
_DIL_HEADS_PER_GROUP = 4
_ALIBI_MAX_EXP = 8.0
_EPS = 1e-6
_N_MIXERS = 2
_LOG2E = 1.4426950408889634

_LANES = 128
_TM = 512
_TM_SB = 1024
_TM_DIL = 1024
_TN = 512
_TQ = 1024
_TK = 256
_SB_HEADS_PER_STEP = 4
_DEC_PAGES_PER_STEP = 8
_DIL_BLOCKS_PER_STEP = 4
_VMEM_LIMIT = 56 * 1024 * 1024


def _dot(a, b):
  return jnp.dot(a, b, preferred_element_type=_F32)


def _dot_nt(a, b):
  return lax.dot_general(a, b, (((1,), (1,)), ((), ())), preferred_element_type=_F32)


def _silu(g):
  return g / (1.0 + jnp.exp(-g))


def _softplus2_parts(z2):
  sign = jnp.int32(-2 ** 31)
  neg_abs = lax.bitcast_convert_type(lax.bitcast_convert_type(z2, jnp.int32) | sign, _F32)
  sp2 = jnp.maximum(z2, 0.0) + jnp.log(1.0 + jnp.exp2(neg_abs)) * _LOG2E
  return sp2, z2 - sp2


def _split_bf16(x):
  hi = x.astype(_BF16)
  lo = (x - hi.astype(_F32)).astype(_BF16)
  return hi, lo


def _rmsnorm_rows(x, g, eps):
  ms = jnp.mean(x * x, axis=-1, keepdims=True)
  return x * lax.rsqrt(ms + eps) * g


def _proj_sb_kernel(x_ref, g_ref, w_ref, q_ref, kv_ref, kb_ref, vb_ref, gate_ref, h_ref, *, ns):
  j = pl.program_id(1)

  @pl.when(j == 0)
  def _():
    h_ref[...] = _rmsnorm_rows(x_ref[...], g_ref[...], _EPS).astype(_BF16)

  acc = _dot(h_ref[...], w_ref[...])

  @pl.when(j < ns)
  def _():
    q_ref[...] = (acc * (_HEAD_DIM ** -0.5 * _LOG2E)).astype(_BF16)

  @pl.when((j >= ns) & (j < 3 * ns))
  def _():
    kv_ref[...] = acc

  @pl.when((j >= ns) & (j < 2 * ns))
  def _():
    kb_ref[...] = acc.astype(_BF16)

  @pl.when((j >= 2 * ns) & (j < 3 * ns))
  def _():
    vb_ref[...] = acc.astype(_BF16)

  @pl.when(j >= 3 * ns)
  def _():
    gate_ref[...] = acc


def _proj_sb(x, g, w):
  m, d = x.shape
  width = w.shape[1] // 4
  tm, tn = min(_TM_SB, m), min(_TN, width)
  ns = width // tn
  clip = lambda j, lo, n: jnp.clip(j - lo, 0, n - 1)
  return pl.pallas_call(
      functools.partial(_proj_sb_kernel, ns=ns),
      grid=(m // tm, 4 * ns),
      in_specs=[
          pl.BlockSpec((tm, d), lambda i, j: (i, 0)),
          pl.BlockSpec((1, d), lambda i, j: (0, 0)),
          pl.BlockSpec((d, tn), lambda i, j: (0, j)),
      ],
      out_specs=[
          pl.BlockSpec((tm, tn), lambda i, j: (i, clip(j, 0, ns))),
          pl.BlockSpec((tm, tn), lambda i, j: (i, clip(j, ns, 2 * ns))),
          pl.BlockSpec((tm, tn), lambda i, j: (i, clip(j, ns, ns))),
          pl.BlockSpec((tm, tn), lambda i, j: (i, clip(j, 2 * ns, ns))),
          pl.BlockSpec((tm, tn), lambda i, j: (i, clip(j, 3 * ns, ns))),
      ],
      out_shape=[
          jax.ShapeDtypeStruct((m, width), _BF16),
          jax.ShapeDtypeStruct((m, 2 * width), _F32),
          jax.ShapeDtypeStruct((m, width), _BF16),
          jax.ShapeDtypeStruct((m, width), _BF16),
          jax.ShapeDtypeStruct((m, width), _F32),
      ],
      scratch_shapes=[pltpu.VMEM((tm, d), _BF16)],
      compiler_params=pltpu.CompilerParams(
          dimension_semantics=("parallel", "arbitrary"), vmem_limit_bytes=_VMEM_LIMIT),
      name="proj_sb",
  )(x, g, w)


def _sb_attn_kernel(bias_ref, q_ref, k_ref, v_ref, gate_ref, o_ref, *, tq, tk, hp):
  hb = pl.program_id(1)
  qi = pl.program_id(2)
  dh = _HEAD_DIM
  nd = tq // tk
  row = lax.broadcasted_iota(jnp.int32, (tk, tk), 0)
  col = lax.broadcasted_iota(jnp.int32, (tk, tk), 1)
  cum = jnp.where(row > col, 1.0, 0.0).astype(_BF16)
  heads = [slice(c * dh, (c + 1) * dh) for c in range(hp)]
  bias2 = [bias_ref[hb * hp + c] * _LOG2E for c in range(hp)]

  def block(j, state, u):
    r0 = 0 if u is None else u * tk
    start = pl.multiple_of(j * tk, tk)
    out = []
    for c, cols in enumerate(heads):
      carry, acc = state[c]
      kb = k_ref[0, pl.ds(start, tk), cols]
      vb = v_ref[0, pl.ds(start, tk), cols]
      z2 = _dot_nt(q_ref[0, r0:, cols], kb) + bias2[c]
      sp2, ls2 = _softplus2_parts(z2)
      if u is not None:
        qrow = lax.broadcasted_iota(jnp.int32, (tq - r0, tk), 0)
        kcol = lax.broadcasted_iota(jnp.int32, (tq - r0, tk), 1)
        strict = kcol < qrow
        sp2 = jnp.where(strict, sp2, 0.0)
      res = _dot(sp2.astype(_BF16), cum)
      a = jnp.exp2(ls2 - res)
      if u is not None:
        a = jnp.where(strict, a, 0.0)
      pv = _dot(a.astype(_BF16), vb) * jnp.exp2(-carry[r0:])
      new_carry = carry[r0:] + res[:, :1] + sp2[:, :1]
      if r0:
        pv = jnp.concatenate([jnp.zeros((r0, dh), _F32), pv], axis=0)
        new_carry = jnp.concatenate([carry[:r0], new_carry], axis=0)
      out.append((new_carry, acc + pv))
    return tuple(out)

  state = tuple((jnp.zeros((tq, 1), _F32), jnp.zeros((tq, dh), _F32)) for _ in heads)
  for u in reversed(range(nd)):
    state = block(qi * nd + u, state, u)
  state = lax.fori_loop(0, qi * nd, lambda it, st: block(qi * nd - 1 - it, st, None), state)
  for c, cols in enumerate(heads):
    o_ref[0, :, cols] = (state[c][1] * _silu(gate_ref[0, :, cols])).astype(_BF16)


def _sb_attn(bias, q, k, v, gate, n_heads):
  b, s, width = q.shape
  tq = min(_TQ, s)
  tk = min(_TK, tq)
  hp = min(_SB_HEADS_PER_STEP, n_heads)
  w = hp * _HEAD_DIM
  return pl.pallas_call(
      functools.partial(_sb_attn_kernel, tq=tq, tk=tk, hp=hp),
      grid=(b, n_heads // hp, s // tq),
      in_specs=[
          pl.BlockSpec(memory_space=pltpu.SMEM),
          pl.BlockSpec((1, tq, w), lambda bi, h, qi: (bi, qi, h)),
          pl.BlockSpec((1, s, w), lambda bi, h, qi: (bi, 0, h)),
          pl.BlockSpec((1, s, w), lambda bi, h, qi: (bi, 0, h)),
          pl.BlockSpec((1, tq, w), lambda bi, h, qi: (bi, qi, h)),
      ],
      out_specs=pl.BlockSpec((1, tq, w), lambda bi, h, qi: (bi, qi, h)),
      out_shape=jax.ShapeDtypeStruct((b, s, width), _BF16),
      compiler_params=pltpu.CompilerParams(
          dimension_semantics=("parallel", "parallel", "arbitrary"),
          vmem_limit_bytes=_VMEM_LIMIT),
      name="sb_attn",
  )(bias, q, k, v, gate)


def _outproj_kernel(a_ref, w_ref, x_ref, o_ref):
  o_ref[...] = x_ref[...] + _dot(a_ref[...], w_ref[...])


def _outproj(a, w, x):
  m, k = a.shape
  n = w.shape[1]
  tm = min(_TM, m)
  return pl.pallas_call(
      _outproj_kernel,
      grid=(m // tm,),
      in_specs=[
          pl.BlockSpec((tm, k), lambda i: (i, 0)),
          pl.BlockSpec((k, n), lambda i: (0, 0)),
          pl.BlockSpec((tm, n), lambda i: (i, 0)),
      ],
      out_specs=pl.BlockSpec((tm, n), lambda i: (i, 0)),
      out_shape=jax.ShapeDtypeStruct((m, n), _F32),
      compiler_params=pltpu.CompilerParams(
          dimension_semantics=("parallel",), vmem_limit_bytes=_VMEM_LIMIT),
      name="outproj",
  )(a, w, x)


def _sb_dec_kernel(pt_ref, bias_ref, qp_ref, kn_ref, vn_ref, gate_ref, cache_ref, o_ref,
                   acc_ref, c_ref, kv_buf, sem, *, n_heads, n_tok, pages_per_step, page_rows,
                   layer_idx):
  seq = pl.program_id(0)
  step = pl.program_id(1)
  n_steps = pl.num_programs(1)
  n_pages = n_steps * pages_per_step
  n_pairs = n_heads // 2
  pr = 2 * n_tok
  rows = n_pairs * pr
  dh = _HEAD_DIM
  n_kv = 2 * n_heads

  def page_copies(w, slot):
    b_w = w // n_steps
    s_w = w % n_steps
    out = []
    for u in range(pages_per_step):
      page = pt_ref[b_w, n_pages - (s_w + 1) * pages_per_step + u]
      for r in range(n_kv):
        out.append(pltpu.make_async_copy(
            cache_ref.at[page, layer_idx, :, r, :],
            kv_buf.at[slot, r, pl.ds(u * page_rows, page_rows), :], sem.at[slot]))
    return out

  w = seq * n_steps + step
  slot = w % 2

  @pl.when(w == 0)
  def _():
    for i, cp in enumerate(page_copies(w, slot)):
      cp.start(priority=i % 2)

  @pl.when(w + 1 < pl.num_programs(0) * n_steps)
  def _():
    for i, cp in enumerate(page_copies(w + 1, 1 - slot)):
      cp.start(priority=i % 2)

  r2 = lax.broadcasted_iota(jnp.int32, (page_rows, 2 * page_rows), 0)
  c2 = lax.broadcasted_iota(jnp.int32, (page_rows, 2 * page_rows), 1)
  cum_ones = jnp.where((c2 >= page_rows) | (r2 > c2), 1.0, 0.0).astype(_BF16)
  top_o = lax.broadcasted_iota(jnp.int32, (pr, dh), 0) < n_tok

  def process(k_of, v_of, nb, mask, carry, accs):
    nk = nb * page_rows
    top = lax.broadcasted_iota(jnp.int32, (pr, nk), 0) < n_tok
    bias_col = jnp.concatenate(
        [jnp.full((n_tok, nk), bias_ref[hh] * _LOG2E, _F32) for hh in range(n_heads)], axis=0)
    zs = []
    for p in range(n_pairs):
      qq = qp_ref[0, p]
      zs.append(jnp.where(top, _dot_nt(qq, k_of(2 * p)), _dot_nt(qq, k_of(2 * p + 1))))
    z2 = jnp.concatenate(zs, axis=0) + bias_col
    sp2, ls2 = _softplus2_parts(z2)
    if mask is not None:
      sp2 = jnp.where(mask, sp2, 0.0)
    hi, lo = _split_bf16(sp2)
    sub = lambda x: [x[:, b * page_rows:(b + 1) * page_rows] for b in range(nb)]
    res = _dot(jnp.concatenate(sub(hi) + sub(lo), axis=0), cum_ones)
    res = res[:nb * rows] + res[nb * rows:]
    tots = [None] * nb
    for b in reversed(range(nb)):
      blk = res[b * rows:(b + 1) * rows]
      tots[b] = blk[:, :page_rows] + carry
      carry = carry + blk[:, page_rows:]
    a = jnp.exp2(ls2 - jnp.concatenate(tots, axis=1))
    if mask is not None:
      a = jnp.where(mask, a, 0.0)
    ab = a.astype(_BF16)
    new_accs = []
    for p in range(n_pairs):
      ap = ab[p * pr:(p + 1) * pr]
      new_accs.append(
          accs[p] + jnp.where(top_o, _dot(ap, v_of(2 * p)), _dot(ap, v_of(2 * p + 1))))
    return carry, new_accs

  @pl.when(step == 0)
  def _():
    tok = lax.broadcasted_iota(jnp.int32, (rows, page_rows), 0) % n_tok
    key = lax.broadcasted_iota(jnp.int32, (rows, page_rows), 1)
    carry, accs = process(
        lambda hh: kn_ref[0, :, hh * dh:(hh + 1) * dh],
        lambda hh: vn_ref[0, :, hh * dh:(hh + 1) * dh],
        1, key < tok, jnp.zeros((rows, page_rows), _F32),
        [jnp.zeros((pr, dh), _F32) for _ in range(n_pairs)])
    c_ref[...] = carry
    for p in range(n_pairs):
      acc_ref[p] = accs[p]

  for cp in page_copies(w, slot):
    cp.wait()

  carry, accs = process(
      lambda hh: kv_buf[slot, hh].astype(_BF16),
      lambda hh: kv_buf[slot, n_heads + hh].astype(_BF16),
      pages_per_step, None, c_ref[...], [acc_ref[p] for p in range(n_pairs)])
  c_ref[...] = carry
  for p in range(n_pairs):
    acc_ref[p] = accs[p]

  @pl.when(step == pl.num_programs(1) - 1)
  def _():
    for p in range(n_pairs):
      o_ref[0, p] = (accs[p] * _silu(gate_ref[0, p])).astype(_BF16)


def _sb_dec(page_table, bias, qp, kn, vn, gate_p, cache, layer_idx, n_heads, n_tok):
  db, n_pages = page_table.shape
  page_rows = cache.shape[2]
  pps = _DEC_PAGES_PER_STEP
  n_pairs = n_heads // 2
  pr = 2 * n_tok
  dh = _HEAD_DIM
  width = n_heads * dh
  per_seq = lambda shape: pl.BlockSpec((1,) + shape, lambda bi, s, pt: (bi,) + (0,) * len(shape))
  return pl.pallas_call(
      functools.partial(_sb_dec_kernel, n_heads=n_heads, n_tok=n_tok, pages_per_step=pps,
                        page_rows=page_rows, layer_idx=layer_idx),
      grid_spec=pltpu.PrefetchScalarGridSpec(
          num_scalar_prefetch=1,
          grid=(db, n_pages // pps),
          in_specs=[
              pl.BlockSpec(memory_space=pltpu.SMEM),
              per_seq((n_pairs, pr, dh)),
              per_seq((page_rows, width)),
              per_seq((page_rows, width)),
              per_seq((n_pairs, pr, dh)),
              pl.BlockSpec(memory_space=pl.ANY),
          ],
          out_specs=per_seq((n_pairs, pr, dh)),
          scratch_shapes=[
              pltpu.VMEM((n_pairs, pr, dh), _F32),
              pltpu.VMEM((n_pairs * pr, page_rows), _F32),
              pltpu.VMEM((2, 2 * n_heads, pps * page_rows, dh), _F32),
              pltpu.SemaphoreType.DMA((2,)),
          ],
      ),
      out_shape=jax.ShapeDtypeStruct((db, n_pairs, pr, dh), _BF16),
      compiler_params=pltpu.CompilerParams(
          dimension_semantics=("arbitrary", "arbitrary"), vmem_limit_bytes=_VMEM_LIMIT),
      name="sb_dec",
  )(page_table, bias, qp, kn, vn, gate_p, cache)


def _rmsnorm_kernel(x_ref, g_ref, h_ref):
  h_ref[...] = _rmsnorm_rows(x_ref[...], g_ref[...], _EPS).astype(_BF16)


def _rmsnorm(x, g):
  m, d = x.shape
  tm = min(_TM_DIL, m)
  return pl.pallas_call(
      _rmsnorm_kernel,
      grid=(m // tm,),
      in_specs=[pl.BlockSpec((tm, d), lambda i: (i, 0)), pl.BlockSpec((1, d), lambda i: (0, 0))],
      out_specs=pl.BlockSpec((tm, d), lambda i: (i, 0)),
      out_shape=jax.ShapeDtypeStruct((m, d), _BF16),
      compiler_params=pltpu.CompilerParams(
          dimension_semantics=("parallel",), vmem_limit_bytes=_VMEM_LIMIT),
      name="rmsnorm",
  )(x, g)


def _proj_dil_kernel(h_ref, gq_ref, gk_ref, w_ref, *rest, dils, hg):
  ng = len(dils)
  q_refs = rest[:ng]
  k_refs = rest[ng:2 * ng]
  v_refs = rest[2 * ng:3 * ng]
  kvw_ref, gate_ref, slab_ref = rest[3 * ng:]
  j = pl.program_id(0)
  dh = _HEAD_DIM
  tm = h_ref.shape[0]

  acc = _dot(h_ref[...], w_ref[...])

  def emit(dst_ref, d, heads):
    for c, xh in enumerate(heads):
      cols = slice(c * dh, (c + 1) * dh)
      if d == 1:
        dst_ref[0, 0, :, cols] = xh.astype(_BF16)
      else:
        slab_ref[c] = xh
        for r in range(d):
          dst_ref[0, r, :, cols] = slab_ref[c, pl.ds(r, tm // d, stride=d), :].astype(_BF16)

  def heads_of(norm_ref):
    out = []
    for c in range(hg):
      xh = acc[:, c * dh:(c + 1) * dh]
      if norm_ref is not None:
        xh = _rmsnorm_rows(xh, norm_ref[...], _EPS)
      out.append(xh)
    return out

  for g in range(ng):
    @pl.when(j == g)
    def _(g=g):
      emit(q_refs[g], dils[g], heads_of(gq_ref))

    @pl.when(j == ng + g)
    def _(g=g):
      heads = heads_of(gk_ref)
      for c, xh in enumerate(heads):
        kvw_ref[:, c * dh:(c + 1) * dh] = xh
      emit(k_refs[g], dils[g], heads)

    @pl.when(j == 2 * ng + g)
    def _(g=g):
      heads = heads_of(None)
      for c, xh in enumerate(heads):
        kvw_ref[:, c * dh:(c + 1) * dh] = xh
      emit(v_refs[g], dils[g], heads)

  @pl.when(j == 3 * ng)
  def _():
    gate_ref[...] = acc


def _proj_dil(x, g, gq, gk, w, dils, rows_per_batch):
  m, d = x.shape
  ng = len(dils)
  hg = _DIL_HEADS_PER_GROUP
  gw = hg * _HEAD_DIM
  tm = min(_TM_DIL, rows_per_batch)
  nb = m // rows_per_batch
  tiles_per_batch = rows_per_batch // tm
  ni = m // tm
  assert w.shape[1] == (3 * ng + 1) * gw and _TN == gw
  h = _rmsnorm(x, g)

  def tile_of(j, i, section):
    return jnp.where(j < section, 0, jnp.where(j == section, i, ni - 1))

  def cls_spec(dil, section):
    def index(j, i):
      t = tile_of(j, i, section)
      return (t // tiles_per_batch, 0, t % tiles_per_batch, 0)
    return pl.BlockSpec((1, dil, tm // dil, gw), index)

  def kvw_index(j, i):
    jj = jnp.clip(j, ng, 3 * ng - 1)
    t = jnp.where(j < ng, 0, jnp.where(j < 3 * ng, i, ni - 1))
    col = jnp.where(jj < 2 * ng, 2 * (jj - ng), 2 * (jj - 2 * ng) + 1)
    return (t, col)

  cls_shape = lambda dil: jax.ShapeDtypeStruct((nb, dil, rows_per_batch // dil, gw), _BF16)
  outs = pl.pallas_call(
      functools.partial(_proj_dil_kernel, dils=dils, hg=hg),
      grid=(3 * ng + 1, ni),
      in_specs=[
          pl.BlockSpec((tm, d), lambda j, i: (i, 0)),
          pl.BlockSpec((1, _HEAD_DIM), lambda j, i: (0, 0)),
          pl.BlockSpec((1, _HEAD_DIM), lambda j, i: (0, 0)),
          pl.BlockSpec((d, gw), lambda j, i: (0, j)),
      ],
      out_specs=[cls_spec(dl, sec * ng + gi) for sec in range(3) for gi, dl in enumerate(dils)] + [
          pl.BlockSpec((tm, gw), kvw_index),
          pl.BlockSpec((tm, gw), lambda j, i: (tile_of(j, i, 3 * ng), 0)),
      ],
      out_shape=[cls_shape(dl) for dl in dils] * 3 + [
          jax.ShapeDtypeStruct((m, 2 * ng * gw), _F32),
          jax.ShapeDtypeStruct((m, gw), _F32),
      ],
      scratch_shapes=[pltpu.VMEM((hg, tm, _HEAD_DIM), _F32)],
      compiler_params=pltpu.CompilerParams(
          dimension_semantics=("arbitrary", "arbitrary"), vmem_limit_bytes=_VMEM_LIMIT),
      name="proj_dil",
  )(h, gq, gk, w)
  return outs[:ng], outs[ng:2 * ng], outs[2 * ng:3 * ng], outs[3 * ng], outs[3 * ng + 1]


def _dil_attn_kernel(slope_ref, q_ref, kp_ref, kc_ref, vp_ref, vc_ref, o_ref, lse_ref,
                     *, dil, group, hg, blk, mb, scale):
  mstep = pl.program_id(1)
  r = pl.program_id(2)
  dh = _HEAD_DIM
  row = lax.broadcasted_iota(jnp.int32, (blk, 2 * blk), 0)
  col = lax.broadcasted_iota(jnp.int32, (blk, 2 * blk), 1)
  delta = blk + row - col
  in_band = (delta >= 0) & (delta <= blk)
  dist = (delta * dil).astype(_F32)
  for u in range(mb):
    valid = in_band & ((col >= blk) | (mstep > 0)) if u == 0 else in_band
    rows_u = slice(u * blk, (u + 1) * blk)
    for c in range(hg):
      cols = slice(c * dh, (c + 1) * dh)
      if u == 0:
        kprev, vprev = kp_ref[0, 0, :, cols], vp_ref[0, 0, :, cols]
      else:
        before = slice((u - 1) * blk, u * blk)
        kprev, vprev = kc_ref[0, 0, before, cols], vc_ref[0, 0, before, cols]
      kk = jnp.concatenate([kprev, kc_ref[0, 0, rows_u, cols]], axis=0)
      vv = jnp.concatenate([vprev, vc_ref[0, 0, rows_u, cols]], axis=0)
      s = _dot_nt(q_ref[0, 0, rows_u, cols], kk) * scale - slope_ref[group * hg + c] * dist
      s = jnp.where(valid, s, -jnp.inf)
      mx = jnp.max(s, axis=1, keepdims=True)
      p = jnp.exp(s - mx)
      den = jnp.sum(p, axis=1, keepdims=True)
      o = _dot((p / den).astype(_BF16), vv)
      lse = jnp.broadcast_to(mx + jnp.log(den), (blk, dh))
      if dil == 1:
        o_ref[0, c, rows_u, :] = o
        lse_ref[0, c, rows_u, :] = lse
      else:
        dst = pl.ds(u * blk * dil + r, blk, stride=dil)
        o_ref[0, c, dst, :] = o
        lse_ref[0, c, dst, :] = lse


def _dil_attn(slopes, q, k, v, dil, group, blk):
  b, _, sd, gw = q.shape
  hg = _DIL_HEADS_PER_GROUP
  dh = _HEAD_DIM
  s = sd * dil
  mb = min(_DIL_BLOCKS_PER_STEP, sd // blk)
  cur = pl.BlockSpec((1, 1, mb * blk, gw), lambda bi, m, r: (bi, r, m, 0))
  prev = pl.BlockSpec((1, 1, blk, gw), lambda bi, m, r: (bi, r, jnp.maximum(m * mb - 1, 0), 0))
  out = pl.BlockSpec((1, hg, mb * blk * dil, dh), lambda bi, m, r: (bi, 0, m, 0))
  return pl.pallas_call(
      functools.partial(_dil_attn_kernel, dil=dil, group=group, hg=hg, blk=blk, mb=mb,
                        scale=dh ** -0.5),
      grid=(b, sd // (mb * blk), dil),
      in_specs=[pl.BlockSpec(memory_space=pltpu.SMEM), cur, prev, cur, prev, cur],
      out_specs=[out, out],
      out_shape=[jax.ShapeDtypeStruct((b, hg, s, dh), _F32)] * 2,
      compiler_params=pltpu.CompilerParams(
          dimension_semantics=("parallel", "arbitrary", "arbitrary"),
          vmem_limit_bytes=_VMEM_LIMIT),
      name=f"dil_attn_g{group}",
  )(slopes, q, k, k, v, v)


def _dil_out_kernel(*refs, ng, hg):
  o_refs = refs[0:2 * ng:2]
  l_refs = refs[1:2 * ng:2]
  gate_ref, w_ref, x_ref, y_ref = refs[2 * ng:]
  dh = _HEAD_DIM
  parts = []
  for c in range(hg):
    lses = [l_refs[g][0, c] for g in range(ng)]
    mx = functools.reduce(jnp.maximum, lses)
    es = [jnp.exp(l - mx) for l in lses]
    den = functools.reduce(jnp.add, es)
    comb = functools.reduce(jnp.add, [(es[g] / den) * o_refs[g][0, c] for g in range(ng)])
    parts.append((comb * _silu(gate_ref[:, c * dh:(c + 1) * dh])).astype(_BF16))
  y_ref[...] = x_ref[...] + _dot(jnp.concatenate(parts, axis=1), w_ref[...])


def _dil_out(os_, lses, gate, w, x, rows_per_batch):
  m, n = x.shape
  ng = len(os_)
  hg = _DIL_HEADS_PER_GROUP
  dh = _HEAD_DIM
  tm = min(_TM, rows_per_batch)
  tpb = rows_per_batch // tm
  head_major = pl.BlockSpec((1, hg, tm, dh), lambda i: (i // tpb, 0, i % tpb, 0))
  args = []
  for o, l in zip(os_, lses):
    args += [o, l]
  return pl.pallas_call(
      functools.partial(_dil_out_kernel, ng=ng, hg=hg),
      grid=(m // tm,),
      in_specs=[head_major] * (2 * ng) + [
          pl.BlockSpec((tm, hg * dh), lambda i: (i, 0)),
          pl.BlockSpec((hg * dh, n), lambda i: (0, 0)),
          pl.BlockSpec((tm, n), lambda i: (i, 0)),
      ],
      out_specs=pl.BlockSpec((tm, n), lambda i: (i, 0)),
      out_shape=jax.ShapeDtypeStruct((m, n), _F32),
      compiler_params=pltpu.CompilerParams(
          dimension_semantics=("parallel",), vmem_limit_bytes=_VMEM_LIMIT),
      name="dil_out",
  )(*args, gate, w, x)


def _dil_dec_kernel(slope_ref, qp_ref, kn_ref, vn_ref, gate_ref, *rest, groups, hg, n_tok, pad_rows):
  ng = len(groups)
  new_refs = rest[:ng]
  st_refs = rest[ng:2 * ng]
  o_ref = rest[-1 - ng]
  win_refs = rest[-ng:]
  dh = _HEAD_DIM
  pr = 2 * n_tok
  n_pairs = hg // 2
  tok_col = lax.broadcasted_iota(jnp.int32, (pr, 1), 0) % n_tok
  top_col = lax.broadcasted_iota(jnp.int32, (pr, 1), 0) < n_tok
  top = lax.broadcasted_iota(jnp.int32, (pr, dh), 0) < n_tok
  row_stride = 2 * hg
  grp = 2 * n_tok
  half = n_tok * row_stride

  for g, (win, _) in enumerate(groups):
    st, dst = st_refs[g], win_refs[g]
    n_grp = win // grp
    cg = min(8, n_grp)

    def shift(m0, n_up, st=st, dst=dst):
      dst[0, 0, pl.ds(m0, cg), :half, :] = st[0, 0, pl.ds(m0, cg), half:, :]
      dst[0, 0, pl.ds(m0, n_up), half:, :] = st[0, 0, pl.ds(m0 + 1, n_up), :half, :]

    if n_grp > cg:
      def body(i, carry, shift=shift):
        shift(i * cg, cg)
        return carry
      lax.fori_loop(0, n_grp // cg - 1, body, 0)
    shift(n_grp - cg, cg - 1)
    dst[0, 0, n_grp - 1, half:, :] = new_refs[g][0]

  for p in range(n_pairs):
    outs, lses = [], []
    for g, (win, dil) in enumerate(groups):
      st = st_refs[g]
      qq = qp_ref[0, g, p]
      ha, hb = 2 * p, 2 * p + 1
      slope = jnp.where(top_col, slope_ref[g * hg + ha], slope_ref[g * hg + hb])
      n_cls = n_tok if dil % grp == 0 else grp
      n_keys = (win // grp) * n_cls

      def buf(off, st=st, n_cls=n_cls, n_keys=n_keys):
        rows = st[0, 0, :, pl.ds(off, n_cls, stride=row_stride), :]
        return rows.reshape(n_keys, dh).astype(_BF16)

      def new(ref, hh, g=g):
        return ref[0, :, (g * hg + hh) * dh:(g * hg + hh + 1) * dh]

      bcol = lax.broadcasted_iota(jnp.int32, (pr, n_keys), 1)
      brow = (bcol >> (n_cls.bit_length() - 1)) * grp + (bcol & (n_cls - 1))
      bdist = win + tok_col - brow
      bvalid = (brow >= tok_col) & ((bdist & (dil - 1)) == 0)
      sb_ = jnp.where(top_col, _dot_nt(qq, buf(ha)), _dot_nt(qq, buf(hb)))
      sb_ = sb_ * (dh ** -0.5) - slope * bdist.astype(_F32)
      sb_ = jnp.where(bvalid, sb_, -jnp.inf)
      ncol = lax.broadcasted_iota(jnp.int32, (pr, pad_rows), 1)
      ndist = tok_col - ncol
      nvalid = (ndist >= 0) & ((ndist & (dil - 1)) == 0)
      sn = jnp.where(top_col, _dot_nt(qq, new(kn_ref, ha)), _dot_nt(qq, new(kn_ref, hb)))
      sn = sn * (dh ** -0.5) - slope * ndist.astype(_F32)
      sn = jnp.where(nvalid, sn, -jnp.inf)

      mx = jnp.maximum(jnp.max(sb_, axis=1, keepdims=True), jnp.max(sn, axis=1, keepdims=True))
      pb = jnp.exp(sb_ - mx)
      pn = jnp.exp(sn - mx)
      den = jnp.sum(pb, axis=1, keepdims=True) + jnp.sum(pn, axis=1, keepdims=True)
      pb = (pb / den).astype(_BF16)
      pn = (pn / den).astype(_BF16)
      o = jnp.where(top, _dot(pb, buf(hg + ha)) + _dot(pn, new(vn_ref, ha)),
                    _dot(pb, buf(hg + hb)) + _dot(pn, new(vn_ref, hb)))
      outs.append(o)
      lses.append(mx + jnp.log(den))
    mx = functools.reduce(jnp.maximum, lses)
    es = [jnp.exp(l - mx) for l in lses]
    den = functools.reduce(jnp.add, es)
    comb = functools.reduce(jnp.add, [(es[g] / den) * outs[g] for g in range(ng)])
    o_ref[0, p] = (comb * _silu(gate_ref[0, p])).astype(_BF16)


def _dil_dec(slopes, qp, kn, vn, gate_p, new_rows, states, prev_wins, layer_idx, groups, n_tok):
  db = qp.shape[0]
  hg = _DIL_HEADS_PER_GROUP
  dh = _HEAD_DIM
  ng = len(groups)
  n_pairs = hg // 2
  pr = 2 * n_tok
  pad_rows = kn.shape[1]
  per_seq = lambda shape: pl.BlockSpec((1,) + shape, lambda bi: (bi,) + (0,) * len(shape))
  st_specs = [
      pl.BlockSpec((1, 1) + st.shape[2:], lambda bi: (bi, layer_idx, 0, 0, 0)) for st in states]
  in_specs = [
      pl.BlockSpec(memory_space=pltpu.SMEM),
      per_seq(qp.shape[1:]),
      per_seq(kn.shape[1:]),
      per_seq(vn.shape[1:]),
      per_seq((n_pairs, pr, dh)),
  ] + [per_seq(nr.shape[1:]) for nr in new_rows] + st_specs
  args = [slopes, qp, kn, vn, gate_p, *new_rows, *states]
  aliases = {}
  if prev_wins is not None:
    for g, pw in enumerate(prev_wins):
      aliases[len(args)] = 1 + g
      args.append(pw)
      in_specs.append(pl.BlockSpec(memory_space=pl.ANY))
  outs = pl.pallas_call(
      functools.partial(_dil_dec_kernel, groups=groups, hg=hg, n_tok=n_tok, pad_rows=pad_rows),
      grid=(db,),
      in_specs=in_specs,
      out_specs=[per_seq((n_pairs, pr, dh))] + st_specs,
      out_shape=[jax.ShapeDtypeStruct((db, n_pairs, pr, dh), _BF16)]
      + [jax.ShapeDtypeStruct(st.shape, _F32) for st in states],
      input_output_aliases=aliases,
      compiler_params=pltpu.CompilerParams(
          dimension_semantics=("arbitrary",), vmem_limit_bytes=_VMEM_LIMIT),
      name="dil_dec",
  )(*args)
  return outs[0], outs[1:]


def _to_pairs(a, db, n_tok, n_heads):
  a = a.reshape(db, n_tok, n_heads, _HEAD_DIM).transpose(0, 2, 1, 3)
  return a.reshape(db, n_heads // 2, 2 * n_tok, _HEAD_DIM)


def _from_pairs(a, db, n_tok, n_heads):
  a = a.reshape(db, n_heads, n_tok, _HEAD_DIM).transpose(0, 2, 1, 3)
  return a.reshape(db * n_tok, n_heads * _HEAD_DIM)


def _pad_tokens(a, db, n_tok, rows):
  a = a.reshape(db, n_tok, a.shape[-1])
  return jnp.pad(a, ((0, 0), (0, rows - n_tok), (0, 0)))


def kernel(x_prompt, x_sample, cache_sb, state_win_g0, state_win_g1, state_win_g2, page_table,
           norm_g, w_in_sb, sb_bias, w_out_sb, w_in_dil, q_norm_dil, k_norm_dil, w_out_dil):
  b, s, d_model = x_prompt.shape
  db, n_tok, _ = x_sample.shape
  depth = norm_g.shape[0]
  dh = _HEAD_DIM
  hg = _DIL_HEADS_PER_GROUP
  ng = len(_DIL_GROUPS)
  sb_heads = sb_bias.shape[1]
  page_rows = cache_sb.shape[2]
  states = (state_win_g0, state_win_g1, state_win_g2)
  dils = tuple(dl for _, dl in _DIL_GROUPS)
  for (win, dl), st in zip(_DIL_GROUPS, states):
    assert st.shape[2] == win and win // dl == _LANES and s % (dl * _LANES) == 0
  assert page_table.shape[1] * page_rows >= max(w for w, _ in _DIL_GROUPS)

  xp = x_prompt.reshape(b * s, d_model)
  xs = x_sample.reshape(db * n_tok, d_model)
  cache = cache_sb.reshape(cache_sb.shape[0], cache_sb.shape[1], page_rows, 2 * sb_heads, dh)
  grp = 2 * n_tok
  assert grp * 2 * hg == _LANES
  states_r = [st.reshape(db, st.shape[1], st.shape[2] // grp, grp * 2 * hg, dh) for st in states]
  heads = jnp.arange(1, ng * hg + 1, dtype=_F32)
  slopes = jnp.exp2(-_ALIBI_MAX_EXP * heads / (ng * hg))

  sb_p, sb_s = [], []
  win_p = [[] for _ in range(ng)]
  wins_s = None
  for layer in range(depth):
    idx = layer // _N_MIXERS
    g = norm_g[layer].reshape(1, d_model)
    if layer % _N_MIXERS == 0:
      w_in = w_in_sb[idx].astype(_BF16)
      w_out = w_out_sb[idx].astype(_BF16)
      bias = sb_bias[idx]
      q, kv, kb, vb, gate = _proj_sb(xp, g, w_in)
      r3 = lambda a: a.reshape(b, s, a.shape[-1])
      og = _sb_attn(bias, r3(q), r3(kb), r3(vb), r3(gate), sb_heads)
      xp = _outproj(og.reshape(b * s, -1), w_out, xp)
      sb_p.append(kv.reshape(b, s, 2, sb_heads, dh))
      q, kv, kb, vb, gate = _proj_sb(xs, g, w_in)
      og = _sb_dec(page_table, bias, _to_pairs(q, db, n_tok, sb_heads),
                   _pad_tokens(kb, db, n_tok, page_rows), _pad_tokens(vb, db, n_tok, page_rows),
                   _to_pairs(gate, db, n_tok, sb_heads), cache, idx, sb_heads, n_tok)
      xs = _outproj(_from_pairs(og, db, n_tok, sb_heads), w_out, xs)
      sb_s.append(kv.reshape(db, n_tok, 2, sb_heads, dh))
    else:
      w_in = w_in_dil[idx].astype(_BF16)
      w_out = w_out_dil[idx].astype(_BF16)
      gq = q_norm_dil[idx].reshape(1, dh)
      gk = k_norm_dil[idx].reshape(1, dh)
      qs, ks, vs, kvw, gate = _proj_dil(xp, g, gq, gk, w_in, dils, s)
      os_, lses = [], []
      for gi, (win, dl) in enumerate(_DIL_GROUPS):
        o, lse = _dil_attn(slopes, qs[gi], ks[gi], vs[gi], dl, gi, win // dl)
        os_.append(o)
        lses.append(lse)
      xp = _dil_out(os_, lses, gate, w_out, xp, s)
      kvw = kvw.reshape(b, s, 2 * ng * hg * dh)
      for gi, (win, _) in enumerate(_DIL_GROUPS):
        rows = kvw[:, s - min(win, s):, gi * 2 * hg * dh:(gi + 1) * 2 * hg * dh]
        win_p[gi].append(rows.reshape(b, min(win, s), 2, hg, dh))
      ones = (1,) * ng
      qs, ks, vs, kvw, gate = _proj_dil(xs, g, gq, gk, w_in, ones, db * n_tok)
      cat = lambda parts: jnp.concatenate([a.reshape(db * n_tok, hg * dh) for a in parts], axis=1)
      qp = jnp.stack([_to_pairs(a.reshape(db * n_tok, hg * dh), db, n_tok, hg) for a in qs], axis=1)
      kvw = kvw.reshape(db, n_tok, ng, 2 * hg, dh)
      new_rows = [kvw[:, :, gi].reshape(db, n_tok * 2 * hg, dh) for gi in range(ng)]
      og, wins_s = _dil_dec(slopes, qp, _pad_tokens(cat(ks), db, n_tok, _LANES),
                            _pad_tokens(cat(vs), db, n_tok, _LANES),
                            _to_pairs(gate, db, n_tok, hg), new_rows, states_r, wins_s, idx,
                            _DIL_GROUPS, n_tok)
      xs = _outproj(_from_pairs(og, db, n_tok, hg), w_out, xs)
  outs = [xp.reshape(b, s, d_model), xs.reshape(db, n_tok, d_model),
          jnp.stack(sb_p, axis=1), jnp.stack(sb_s, axis=1)]
  for gi in range(ng):
    outs += [jnp.stack(win_p[gi], axis=1), wins_s[gi].reshape(states[gi].shape)]
  return tuple(outs)
```
